```python
import jax, jax.numpy as jnp
from jax import lax
import numpy as np

D_MODEL = 1024
BATCH = 2
SEQ = 16384
DEPTH = 2

HEAD_DIM = 64
N_MEM_TOKENS = 256
N_MEM_HEADS = 4
MEM_WIDTH = N_MEM_HEADS * HEAD_DIM
MIX_WIDTH = D_MODEL
MIXER_WIDTH = MIX_WIDTH - MEM_WIDTH
CHUNK = 128
GMLP_GROUP_DIM = 128
GMLP_GROUPS = MIXER_WIDTH // GMLP_GROUP_DIM
MOBA_HEADS = MIXER_WIDTH // HEAD_DIM
MOBA_BLOCK = 256
MOBA_TOPK = 3
Q_SUB = 64
D_FF = ((8 * D_MODEL // 3 + 127) // 128) * 128
CONV_WIDTH = 3
N_A_LAYERS = (DEPTH + 1) // 2
N_B_LAYERS = DEPTH // 2
EPS = 1e-6
NEG = -1e30

kernel_name = 'hybrid_gmlp_moba_memxattn_convffn'


def rms_norm(x, g):
    xf = x.astype(jnp.float32)
    y = xf * lax.rsqrt(jnp.mean(xf * xf, axis=-1, keepdims=True) + EPS)
    return (y * g.astype(jnp.float32)).astype(x.dtype)


def mem_cross_attention(q_m, k_m, v_m, g_q):
    B, S, _ = q_m.shape
    q = rms_norm(q_m.reshape(B, S, N_MEM_HEADS, HEAD_DIM), g_q)
    s = jnp.einsum('bshd,bmhd->bhsm', q, k_m).astype(jnp.float32) * (HEAD_DIM ** -0.5)
    p = jax.nn.softmax(s, axis=-1).astype(v_m.dtype)
    o = jnp.einsum('bhsm,bmhd->bshd', p, v_m)
    return o.reshape(B, S, MEM_WIDTH)


def gmlp_spatial_gating(p_mix, g_sgu, w_s, b_s):
    B, S, _ = p_mix.shape
    z = jax.nn.gelu(p_mix)
    u = z[..., :MIXER_WIDTH]
    v = rms_norm(z[..., MIXER_WIDTH:], g_sgu)
    v = v.reshape(B, S // CHUNK, CHUNK, GMLP_GROUPS, GMLP_GROUP_DIM)
    w_causal = w_s * jnp.tril(jnp.ones((CHUNK, CHUNK), w_s.dtype))
    mixed = jnp.einsum('gts,bcsgd->bctgd', w_causal, v) + b_s.T[:, :, None]
    return u * mixed.reshape(B, S, MIXER_WIDTH)


def moba_attention(q, k, v):
    B, H, S, Dh = q.shape
    nb = -(-S // MOBA_BLOCK)
    pad = nb * MOBA_BLOCK - S
    k_blk = jnp.pad(k, ((0, 0), (0, 0), (0, pad), (0, 0))).reshape(B, H, nb, MOBA_BLOCK, Dh)
    v_blk = jnp.pad(v, ((0, 0), (0, 0), (0, pad), (0, 0))).reshape(B, H, nb, MOBA_BLOCK, Dh)
    k_mean = jnp.mean(k_blk.astype(jnp.float32), axis=3)
    gate = jnp.einsum('bhsd,bhnd->bhsn', q.astype(jnp.float32), k_mean)
    q_block = jnp.arange(S) // MOBA_BLOCK
    past = jnp.arange(nb)[None, :] < q_block[:, None]
    gate = jnp.where(past, gate, NEG)
    n_sel = min(MOBA_TOPK, nb)
    _, sel = lax.top_k(gate, n_sel)

    n_sub = S // Q_SUB
    def to_sub(a):
        return jnp.moveaxis(a.reshape((B, H, n_sub, Q_SUB) + a.shape[3:]), 2, 0)
    q_s = to_sub(q)
    sel_s = to_sub(sel)
    t0 = jnp.arange(n_sub, dtype=jnp.int32) * Q_SUB
    b_idx = jnp.arange(B)[:, None, None]
    h_idx = jnp.arange(H)[None, :, None]
    scale = Dh ** -0.5

    def step(args):
        qb, sb, t = args
        own = t // MOBA_BLOCK
        idx = sb.reshape(B, H, Q_SUB * n_sel)
        k_sel = k_blk[b_idx, h_idx, idx].reshape(B, H, Q_SUB, n_sel, MOBA_BLOCK, Dh)
        v_sel = v_blk[b_idx, h_idx, idx].reshape(B, H, Q_SUB, n_sel, MOBA_BLOCK, Dh)
        k_own = lax.dynamic_index_in_dim(k_blk, own, axis=2, keepdims=False)
        v_own = lax.dynamic_index_in_dim(v_blk, own, axis=2, keepdims=False)
        s_sel = jnp.einsum('bhqd,bhqjkd->bhqjk', qb, k_sel).astype(jnp.float32) * scale
        valid = jnp.arange(n_sel) < own
        s_sel = jnp.where(valid[:, None], s_sel, NEG)
        s_own = jnp.einsum('bhqd,bhkd->bhqk', qb, k_own).astype(jnp.float32) * scale
        q_pos = t + jnp.arange(Q_SUB)
        k_pos = own * MOBA_BLOCK + jnp.arange(MOBA_BLOCK)
        s_own = jnp.where(k_pos[None, :] <= q_pos[:, None], s_own, NEG)
        s_all = jnp.concatenate([s_sel.reshape(B, H, Q_SUB, n_sel * MOBA_BLOCK), s_own], axis=-1)
        p = jax.nn.softmax(s_all, axis=-1).astype(v.dtype)
        p_sel = p[..., :n_sel * MOBA_BLOCK].reshape(B, H, Q_SUB, n_sel, MOBA_BLOCK)
        p_own = p[..., n_sel * MOBA_BLOCK:]
        return (jnp.einsum('bhqjk,bhqjkd->bhqd', p_sel, v_sel)
                + jnp.einsum('bhqk,bhkd->bhqd', p_own, v_own))

    out = lax.map(step, (q_s, sel_s, t0))
    return jnp.moveaxis(out, 0, 2).reshape(B, H, S, Dh)


def conv_gated_ffn(xn, w_up, conv_w, conv_b, w_down):
    h = xn @ w_up
    def shift(a, n):
        return jnp.pad(a[:, :-n], ((0, 0), (n, 0), (0, 0)))
    h = conv_w[2] * h + conv_w[1] * shift(h, 1) + conv_w[0] * shift(h, 2) + conv_b
    gate, val = h[..., :D_FF], h[..., D_FF:]
    return (jax.nn.gelu(gate) * val) @ w_down


def setup_inputs(seed: int = 0) -> dict:
    key = jax.random.key(seed)
    ks = jax.random.split(key, 24)
    f32 = jnp.float32
    def nrm(k, shape, fan_in):
        return jax.random.normal(k, shape, f32) * (fan_in ** -0.5)
    def gain(k, shape):
        return 1.0 + 0.01 * jax.random.normal(k, shape, f32)
    return {
        'x': jax.random.normal(ks[0], (BATCH, SEQ, D_MODEL), f32),
        'mem': jax.random.normal(ks[1], (BATCH, N_MEM_TOKENS, D_MODEL), f32),
        'g_mix': gain(ks[2], (DEPTH, D_MODEL)),
        'g_ffn': gain(ks[3], (DEPTH, D_MODEL)),
        'w_in_a': nrm(ks[4], (N_A_LAYERS, D_MODEL, 2 * MIXER_WIDTH + MEM_WIDTH), D_MODEL),
        'w_out_a': nrm(ks[5], (N_A_LAYERS, MIX_WIDTH, D_MODEL), MIX_WIDTH),
        'g_sgu': gain(ks[6], (N_A_LAYERS, MIXER_WIDTH)),
        'w_s': nrm(ks[7], (N_A_LAYERS, GMLP_GROUPS, CHUNK, CHUNK), CHUNK),
        'b_s': gain(ks[8], (N_A_LAYERS, GMLP_GROUPS, CHUNK)),
        'w_in_b': nrm(ks[9], (N_B_LAYERS, D_MODEL, 3 * MIXER_WIDTH + MEM_WIDTH), D_MODEL),
        'w_out_b': nrm(ks[10], (N_B_LAYERS, MIX_WIDTH, D_MODEL), MIX_WIDTH),
        'g_q_b': gain(ks[11], (N_B_LAYERS, HEAD_DIM)),
        'g_k_b': gain(ks[12], (N_B_LAYERS, HEAD_DIM)),
        'g_mem': gain(ks[13], (D_MODEL,)),
        'w_mem_kv': nrm(ks[14], (D_MODEL, 2 * MEM_WIDTH), D_MODEL),
        'g_km': gain(ks[15], (HEAD_DIM,)),
        'g_qm': gain(ks[16], (DEPTH, HEAD_DIM)),
        'w_up': nrm(ks[17], (DEPTH, D_MODEL, 2 * D_FF), D_MODEL),
        'conv_w': nrm(ks[18], (DEPTH, CONV_WIDTH, 2 * D_FF), CONV_WIDTH),
        'conv_b': 0.01 * jax.random.normal(ks[19], (DEPTH, 2 * D_FF), f32),
        'w_down': nrm(ks[20], (DEPTH, D_FF, D_MODEL), D_FF),
    }


def reference(x, mem, g_mix, g_ffn, w_in_a, w_out_a, g_sgu, w_s, b_s,
              w_in_b, w_out_b, g_q_b, g_k_b, g_mem, w_mem_kv, g_km, g_qm,
              w_up, conv_w, conv_b, w_down):
    B, S, _ = x.shape
    kv = rms_norm(mem, g_mem) @ w_mem_kv
    k_m = rms_norm(kv[..., :MEM_WIDTH].reshape(B, N_MEM_TOKENS, N_MEM_HEADS, HEAD_DIM), g_km)
    v_m = kv[..., MEM_WIDTH:].reshape(B, N_MEM_TOKENS, N_MEM_HEADS, HEAD_DIM)

    h = x
    for i in range(DEPTH):
        xn = rms_norm(h, g_mix[i])
        j = i // 2
        if i % 2 == 0:
            p = xn @ w_in_a[j]
            y_mix = gmlp_spatial_gating(p[..., :2 * MIXER_WIDTH], g_sgu[j], w_s[j], b_s[j])
            y_mem = mem_cross_attention(p[..., 2 * MIXER_WIDTH:], k_m, v_m, g_qm[i])
            w_out = w_out_a[j]
        else:
            p = xn @ w_in_b[j]
            def heads(a, g):
                return rms_norm(a.reshape(B, S, MOBA_HEADS, HEAD_DIM), g).transpose(0, 2, 1, 3)
            q = heads(p[..., :MIXER_WIDTH], g_q_b[j])
            k = heads(p[..., MIXER_WIDTH:2 * MIXER_WIDTH], g_k_b[j])
            v = p[..., 2 * MIXER_WIDTH:3 * MIXER_WIDTH].reshape(B, S, MOBA_HEADS, HEAD_DIM).transpose(0, 2, 1, 3)
            o = moba_attention(q, k, v)
            y_mix = o.transpose(0, 2, 1, 3).reshape(B, S, MIXER_WIDTH)
            y_mem = mem_cross_attention(p[..., 3 * MIXER_WIDTH:], k_m, v_m, g_qm[i])
            w_out = w_out_b[j]
        h = h + jnp.concatenate([y_mix, y_mem], axis=-1) @ w_out
        h = h + conv_gated_ffn(rms_norm(h, g_ffn[i]), w_up[i], conv_w[i], conv_b[i], w_down[i])
    return h
```

```python
import functools
import math

import jax
import jax.numpy as jnp
from jax import lax
from jax.experimental import pallas as pl
from jax.experimental.pallas import tpu as pltpu

F32 = jnp.float32
BF16 = jnp.bfloat16

HEAD_DIM = 64
LANES = 128
N_MEM_HEADS = 4
MEM_WIDTH = N_MEM_HEADS * HEAD_DIM
CHUNK = 128
GROUP_DIM = 128
MOBA_BLOCK = 256
MOBA_TOPK = 3
CONV_WIDTH = 3
HALO = 8
EPS = 1e-6
NEG = -1e30
VMEM_LIMIT = 56 * 1024 * 1024


def _nt(a, b):
    return lax.dot_general(a, b, (((1,), (1,)), ((), ())), preferred_element_type=F32)


def _dot(a, b):
    return jnp.dot(a, b, preferred_element_type=F32)


def _gelu(x):
    c = math.sqrt(2.0 / math.pi)
    return 0.5 * x * (1.0 + jnp.tanh(c * (x + 0.044715 * (x * x * x))))


def _rms(x, g):
    return x * lax.rsqrt(jnp.mean(x * x, axis=-1, keepdims=True) + EPS) * g


def _pair_rms(x):
    lane = lax.broadcasted_iota(jnp.int32, (1, LANES), 1)
    lo = lane < HEAD_DIM
    outs = []
    for c in range(x.shape[1] // LANES):
        xc = x[:, c * LANES:(c + 1) * LANES]
        x2 = xc * xc
        s_lo = jnp.sum(jnp.where(lo, x2, 0.0), axis=-1, keepdims=True)
        s_hi = jnp.sum(jnp.where(lo, 0.0, x2), axis=-1, keepdims=True)
        ms = jnp.where(lo, s_lo, s_hi) * (1.0 / HEAD_DIM)
        outs.append(xc * lax.rsqrt(ms + EPS))
    return jnp.concatenate(outs, axis=-1)


def _spread_heads(x):
    lane = lax.broadcasted_iota(jnp.int32, (1, LANES), 1)
    lo = lane < HEAD_DIM
    outs = []
    for c in range(x.shape[1] // LANES):
        xc = x[:, c * LANES:(c + 1) * LANES]
        outs.append(jnp.where(lo, xc, 0.0))
        outs.append(jnp.where(lo, pltpu.roll(xc, HEAD_DIM, 1), 0.0))
    return jnp.concatenate(outs, axis=-1)


def _mem_heads(qm, gq, km_ref, vm_ref):
    q = (_pair_rms(qm) * (gq * HEAD_DIM ** -0.5)).astype(BF16)
    outs = []
    for hh in range(N_MEM_HEADS):
        ls = slice(hh * HEAD_DIM, (hh + 1) * HEAD_DIM)
        s = _nt(q[:, ls], km_ref[0, :, ls])
        e = jnp.exp(s - jnp.max(s, axis=-1, keepdims=True))
        l = jnp.sum(e, axis=-1, keepdims=True)
        outs.append(_dot(e.astype(BF16), vm_ref[0, :, ls]) / l)
    return jnp.concatenate(outs, axis=-1)


def _mem_kv_kernel(mem_ref, g_ref, w_ref, gk_ref, km_ref, vm_ref):
    xn = _rms(mem_ref[0], g_ref[...]).astype(BF16)
    kv = _dot(xn, w_ref[...])
    km_ref[0] = (_pair_rms(kv[:, :MEM_WIDTH]) * gk_ref[...]).astype(BF16)
    vm_ref[0] = kv[:, MEM_WIDTH:].astype(BF16)


def _mem_kv(mem, g_mem, w_mem_kv, g_km):
    B, M, D = mem.shape
    gk = jnp.tile(g_km, N_MEM_HEADS)[None]
    full = lambda shape: pl.BlockSpec(shape, lambda b: (0,) * len(shape))
    return pl.pallas_call(
        _mem_kv_kernel,
        grid=(B,),
        in_specs=[pl.BlockSpec((1, M, D), lambda b: (b, 0, 0)),
                  full((1, D)), full((D, 2 * MEM_WIDTH)), full((1, MEM_WIDTH))],
        out_specs=[pl.BlockSpec((1, M, MEM_WIDTH), lambda b: (b, 0, 0))] * 2,
        out_shape=[jax.ShapeDtypeStruct((B, M, MEM_WIDTH), BF16)] * 2,
        name="mem_kv",
    )(mem, g_mem[None], w_mem_kv.astype(BF16), gk)


def _mixer_a_kernel(h_ref, g_ref, win_ref, gsgu_ref, ws_ref, bias_ref, km_ref, vm_ref,
                    gqm_ref, wout_ref, o_ref, ycat_ref, *, mixer_width):
    x = h_ref[0]
    tm = x.shape[0]
    n_groups = mixer_width // GROUP_DIM
    n_chunks = tm // CHUNK
    xn = _rms(x, g_ref[...]).astype(BF16)
    p = _dot(xn, win_ref[...])
    z = _gelu(p[:, :2 * mixer_width])
    u = z[:, :mixer_width]
    vn = _rms(z[:, mixer_width:], gsgu_ref[...]).astype(BF16)
    row = lax.broadcasted_iota(jnp.int32, (CHUNK, CHUNK), 0)
    col = lax.broadcasted_iota(jnp.int32, (CHUNK, CHUNK), 1)
    causal = row >= col
    for g in range(n_groups):
        gs = slice(g * GROUP_DIM, (g + 1) * GROUP_DIM)
        wc = jnp.where(causal, ws_ref[g], 0.0).astype(BF16)
        vg = jnp.concatenate([vn[c * CHUNK:(c + 1) * CHUNK, gs] for c in range(n_chunks)], axis=1)
        mixed = _dot(wc, vg)
        for c in range(n_chunks):
            rs = slice(c * CHUNK, (c + 1) * CHUNK)
            m_c = mixed[:, c * GROUP_DIM:(c + 1) * GROUP_DIM] + bias_ref[:, gs]
            ycat_ref[rs, gs] = (u[rs, gs] * m_c).astype(BF16)
    y_mem = _mem_heads(p[:, 2 * mixer_width:], gqm_ref[...], km_ref, vm_ref)
    ycat_ref[:, mixer_width:] = y_mem.astype(BF16)
    o_ref[0] = x + _dot(ycat_ref[...], wout_ref[...])


def _mixer_a(h, g_mix, w_in, g_sgu, w_s, b_s, km, vm, g_qm, w_out, *, tm):
    B, S, D = h.shape
    mixer_width = g_sgu.shape[0]
    n_in = w_in.shape[1]
    M = km.shape[1]
    bias = jnp.repeat(b_s.T, GROUP_DIM, axis=1)
    gqm = jnp.tile(g_qm, N_MEM_HEADS)[None]
    full = lambda shape: pl.BlockSpec(shape, lambda b, i: (0,) * len(shape))
    return pl.pallas_call(
        functools.partial(_mixer_a_kernel, mixer_width=mixer_width),
        grid=(B, S // tm),
        in_specs=[pl.BlockSpec((1, tm, D), lambda b, i: (b, i, 0)),
                  full((1, D)), full((D, n_in)), full((1, mixer_width)),
                  full(w_s.shape), full((CHUNK, mixer_width)),
                  pl.BlockSpec((1, M, MEM_WIDTH), lambda b, i: (b, 0, 0)),
                  pl.BlockSpec((1, M, MEM_WIDTH), lambda b, i: (b, 0, 0)),
                  full((1, MEM_WIDTH)), full((D, D))],
        out_specs=pl.BlockSpec((1, tm, D), lambda b, i: (b, i, 0)),
        out_shape=jax.ShapeDtypeStruct((B, S, D), F32),
        scratch_shapes=[pltpu.VMEM((tm, D), BF16)],
        compiler_params=pltpu.CompilerParams(
            dimension_semantics=("parallel", "parallel"), vmem_limit_bytes=VMEM_LIMIT),
        name="mixer_a",
    )(h, g_mix[None], w_in.astype(BF16), g_sgu[None], w_s, bias, km, vm, gqm, w_out.astype(BF16))


def _ffn_kernel(h_ref, halo_ref, g_ref, wup_ref, cw_ref, cb_ref, wdn_ref, o_ref, a_ref,
                *, d_ff, ch):
    i = pl.program_id(1)
    x = h_ref[0]
    tm = x.shape[0]
    g = g_ref[...]
    xn = _rms(x, g).astype(BF16)
    xh = _rms(halo_ref[0], g)
    xh = jnp.where(i > 0, xh, 0.0).astype(BF16)
    xe = jnp.concatenate([xh, xn], axis=0)

    def conv_part(cols):
        hh = _dot(xe, wup_ref[:, cols])
        w = cw_ref[:, cols]
        return (w[2:3] * hh[HALO:] + w[1:2] * hh[HALO - 1:HALO - 1 + tm]
                + w[0:1] * hh[HALO - 2:HALO - 2 + tm] + cb_ref[:, cols])

    for c in range(d_ff // ch):
        gate = conv_part(slice(c * ch, (c + 1) * ch))
        val = conv_part(slice(d_ff + c * ch, d_ff + (c + 1) * ch))
        a_ref[:, c * ch:(c + 1) * ch] = (_gelu(gate) * val).astype(BF16)
    o_ref[0] = x + _dot(a_ref[...], wdn_ref[...])


def _ffn(h, g_ffn, w_up, conv_w, conv_b, w_down, *, tm, ch):
    B, S, D = h.shape
    d_ff = w_down.shape[0]
    full = lambda shape: pl.BlockSpec(shape, lambda b, i: (0,) * len(shape))
    hb = tm // HALO
    return pl.pallas_call(
        functools.partial(_ffn_kernel, d_ff=d_ff, ch=ch),
        grid=(B, S // tm),
        in_specs=[pl.BlockSpec((1, tm, D), lambda b, i: (b, i, 0)),
                  pl.BlockSpec((1, HALO, D), lambda b, i: (b, jnp.maximum(i * hb - 1, 0), 0)),
                  full((1, D)), full((D, 2 * d_ff)), full((CONV_WIDTH, 2 * d_ff)),
                  full((1, 2 * d_ff)), full((d_ff, D))],
        out_specs=pl.BlockSpec((1, tm, D), lambda b, i: (b, i, 0)),
        out_shape=jax.ShapeDtypeStruct((B, S, D), F32),
        scratch_shapes=[pltpu.VMEM((tm, d_ff), BF16)],
        compiler_params=pltpu.CompilerParams(
            dimension_semantics=("parallel", "parallel"), vmem_limit_bytes=VMEM_LIMIT),
        name="ffn",
    )(h, h, g_ffn[None], w_up.astype(BF16), conv_w, conv_b[None], w_down.astype(BF16))


def _inproj_b_kernel(h_ref, g_ref, win_ref, gq_ref, gk_ref, km_ref, vm_ref, gqm_ref,
                     qhi_ref, qlo_ref, k_ref, v_ref, kmean_ref, ymem_ref, *, mixer_width):
    i = pl.program_id(1)
    x = h_ref[0]
    tm = x.shape[0]
    xn = _rms(x, g_ref[...]).astype(BF16)
    p = _dot(xn, win_ref[...])
    q = _spread_heads(p[:, :mixer_width])
    k = _spread_heads(p[:, mixer_width:2 * mixer_width])
    n_heads = mixer_width // HEAD_DIM

    def head_norm(a, gain):
        outs = []
        for h in range(n_heads):
            ah = a[:, h * LANES:(h + 1) * LANES]
            ms = jnp.sum(ah * ah, axis=-1, keepdims=True) * (1.0 / HEAD_DIM)
            outs.append(ah * lax.rsqrt(ms + EPS))
        return jnp.concatenate(outs, axis=-1) * gain

    q = head_norm(q, gq_ref[...] * HEAD_DIM ** -0.5)
    k = head_norm(k, gk_ref[...])
    q_hi = q.astype(BF16)
    qhi_ref[0] = q_hi
    qlo_ref[0] = (q - q_hi.astype(F32)).astype(BF16)
    nb = tm // MOBA_BLOCK
    kmean_ref[0] = jnp.concatenate(
        [jnp.mean(k[r * MOBA_BLOCK:(r + 1) * MOBA_BLOCK], axis=0, keepdims=True) for r in range(nb)],
        axis=0)
    lane = lax.broadcasted_iota(jnp.int32, (tm, LANES), 1)
    blk = (i * tm + lax.broadcasted_iota(jnp.int32, (tm, LANES), 0)) // MOBA_BLOCK
    e = jnp.where(lane - HEAD_DIM == blk, NEG, 0.0)
    lo = lane < HEAD_DIM
    k_ref[0] = jnp.concatenate(
        [jnp.where(lo, k[:, h * LANES:(h + 1) * LANES], e) for h in range(n_heads)],
        axis=-1).astype(BF16)
    v_ref[0] = p[:, 2 * mixer_width:3 * mixer_width].astype(BF16)
    ymem_ref[0] = _mem_heads(p[:, 3 * mixer_width:], gqm_ref[...], km_ref, vm_ref).astype(BF16)


def _pad_gain(g, n_heads):
    return jnp.tile(jnp.concatenate([g, jnp.zeros_like(g)]), n_heads)[None]


def _inproj_b(h, g_mix, w_in, g_q, g_k, km, vm, g_qm, *, tm):
    B, S, D = h.shape
    n_in = w_in.shape[1]
    mixer_width = (n_in - MEM_WIDTH) // 3
    n_heads = mixer_width // HEAD_DIM
    wide = n_heads * LANES
    M = km.shape[1]
    nt = S // tm
    nb = tm // MOBA_BLOCK
    gqm = jnp.tile(g_qm, N_MEM_HEADS)[None]
    full = lambda shape: pl.BlockSpec(shape, lambda b, i: (0,) * len(shape))
    tile = lambda w: pl.BlockSpec((1, tm, w), lambda b, i: (b, i, 0))
    outs = pl.pallas_call(
        functools.partial(_inproj_b_kernel, mixer_width=mixer_width),
        grid=(B, nt),
        in_specs=[tile(D), full((1, D)), full((D, n_in)), full((1, wide)), full((1, wide)),
                  pl.BlockSpec((1, M, MEM_WIDTH), lambda b, i: (b, 0, 0)),
                  pl.BlockSpec((1, M, MEM_WIDTH), lambda b, i: (b, 0, 0)),
                  full((1, MEM_WIDTH))],
        out_specs=[tile(wide), tile(wide), tile(wide), tile(mixer_width),
                   pl.BlockSpec((1, nb, wide), lambda b, i: (b * nt + i, 0, 0)),
                   tile(MEM_WIDTH)],
        out_shape=[jax.ShapeDtypeStruct((B, S, wide), BF16),
                   jax.ShapeDtypeStruct((B, S, wide), BF16),
                   jax.ShapeDtypeStruct((B, S, wide), BF16),
                   jax.ShapeDtypeStruct((B, S, mixer_width), BF16),
                   jax.ShapeDtypeStruct((B * nt, nb, wide), F32),
                   jax.ShapeDtypeStruct((B, S, MEM_WIDTH), BF16)],
        compiler_params=pltpu.CompilerParams(
            dimension_semantics=("parallel", "parallel"), vmem_limit_bytes=VMEM_LIMIT),
        name="inproj_b",
    )(h, g_mix[None], w_in.astype(BF16), _pad_gain(g_q, n_heads), _pad_gain(g_k, n_heads),
      km, vm, gqm)
    q_hi, q_lo, k_aug, v, kmean, y_mem = outs
    return q_hi, q_lo, k_aug, v, kmean.reshape(B, S // MOBA_BLOCK, n_heads, LANES), y_mem


def _moba_kernel(qhi_ref, qlo_ref, k_ref, v_ref, kmp_ref, o_ref):
    own = pl.program_id(2)
    tq = qhi_ref.shape[1]
    lane = lax.broadcasted_iota(jnp.int32, (1, LANES), 1)
    blk = lane - HEAD_DIM
    row = lax.broadcasted_iota(jnp.int32, (tq, MOBA_BLOCK), 0)
    col = lax.broadcasted_iota(jnp.int32, (tq, MOBA_BLOCK), 1)
    causal = col <= row
    outs = []
    for hd in range(2):
        ls = slice(hd * LANES, (hd + 1) * LANES)
        q_hi = qhi_ref[0, :, ls]
        q_lo = qlo_ref[0, :, ls]
        kmp = kmp_ref[0, hd]
        kmp_hi = kmp.astype(BF16)
        kmp_lo = (kmp - kmp_hi.astype(F32)).astype(BF16)
        gate = _nt(q_hi, kmp_hi) + (_nt(q_lo, kmp_hi) + _nt(q_hi, kmp_lo))
        gate = jnp.where(blk < 0, -jnp.inf, jnp.where(blk < own, gate, NEG))
        sel = jnp.zeros(gate.shape, jnp.bool_)
        for j in range(MOBA_TOPK):
            mx = jnp.max(gate, axis=-1, keepdims=True)
            first = jnp.min(jnp.where(gate == mx, lane, LANES), axis=-1, keepdims=True)
            pick = lane == first
            sel = jnp.logical_or(sel, jnp.logical_and(pick, j < own))
            gate = jnp.where(pick, -jnp.inf, gate)
        not_sel = jnp.where(jnp.logical_or(sel, blk < 0), 0.0, 1.0).astype(BF16)
        q_aug = q_hi + not_sel

        k_own = k_ref[0, pl.ds(pl.multiple_of(own * MOBA_BLOCK, MOBA_BLOCK), MOBA_BLOCK), ls]
        v_own = v_ref[0, pl.ds(pl.multiple_of(own * MOBA_BLOCK, MOBA_BLOCK), MOBA_BLOCK), :]
        s = jnp.where(causal, _nt(q_hi, k_own), NEG)
        m0 = jnp.max(s, axis=-1, keepdims=True)
        p = jnp.exp(s - m0)
        l0 = jnp.sum(p, axis=-1, keepdims=True)
        acc0 = _dot(p.astype(BF16), v_own)

        def body(n, carry):
            m, l, acc = carry
            start = pl.multiple_of(n * MOBA_BLOCK, MOBA_BLOCK)
            s = _nt(q_aug, k_ref[0, pl.ds(start, MOBA_BLOCK), ls])
            m_new = jnp.maximum(m, jnp.max(s, axis=-1, keepdims=True))
            alpha = jnp.exp(m - m_new)
            p = jnp.exp(s - m_new)
            l = alpha * l + jnp.sum(p, axis=-1, keepdims=True)
            acc = alpha * acc + _dot(p.astype(BF16), v_ref[0, pl.ds(start, MOBA_BLOCK), :])
            return m_new, l, acc

        _, l, acc = lax.fori_loop(0, own, body, (m0, l0, acc0))
        outs.append(acc / l)
    o_ref[0] = jnp.where(lane < HEAD_DIM, outs[0], outs[1]).astype(o_ref.dtype)


def _moba(q_hi, q_lo, k_aug, v, kmp, *, tq):
    B, S, wide = q_hi.shape
    n_pairs = wide // (2 * LANES)
    return pl.pallas_call(
        _moba_kernel,
        grid=(B, n_pairs, S // tq),
        in_specs=[pl.BlockSpec((1, tq, 2 * LANES), lambda b, hp, t: (b, t, hp)),
                  pl.BlockSpec((1, tq, 2 * LANES), lambda b, hp, t: (b, t, hp)),
                  pl.BlockSpec((1, S, 2 * LANES), lambda b, hp, t: (b, 0, hp)),
                  pl.BlockSpec((1, S, LANES), lambda b, hp, t: (b, 0, hp)),
                  pl.BlockSpec((1, 2, LANES, LANES), lambda b, hp, t: (b, hp, 0, 0))],
        out_specs=pl.BlockSpec((1, tq, LANES), lambda b, hp, t: (b, t, hp)),
        out_shape=jax.ShapeDtypeStruct((B, S, n_pairs * LANES), BF16),
        compiler_params=pltpu.CompilerParams(
            dimension_semantics=("parallel", "parallel", "arbitrary"),
            vmem_limit_bytes=VMEM_LIMIT),
        name="moba",
    )(q_hi, q_lo, k_aug, v, kmp)


def _outproj_kernel(h_ref, ya_ref, ym_ref, w_ref, o_ref):
    wa = ya_ref.shape[2]
    y = _dot(ya_ref[0], w_ref[:wa, :]) + _dot(ym_ref[0], w_ref[wa:, :])
    o_ref[0] = h_ref[0] + y


def _outproj(h, y_attn, y_mem, w_out, *, tm):
    B, S, D = h.shape
    tile = lambda w: pl.BlockSpec((1, tm, w), lambda b, i: (b, i, 0))
    return pl.pallas_call(
        _outproj_kernel,
        grid=(B, S // tm),
        in_specs=[tile(D), tile(y_attn.shape[2]), tile(y_mem.shape[2]),
                  pl.BlockSpec(w_out.shape, lambda b, i: (0, 0))],
        out_specs=tile(D),
        out_shape=jax.ShapeDtypeStruct((B, S, D), F32),
        compiler_params=pltpu.CompilerParams(
            dimension_semantics=("parallel", "parallel"), vmem_limit_bytes=VMEM_LIMIT),
        name="outproj_b",
    )(h, y_attn, y_mem, w_out.astype(BF16))


def kernel(x, mem, g_mix, g_ffn, w_in_a, w_out_a, g_sgu, w_s, b_s, w_in_b, w_out_b, g_q_b,
           g_k_b, g_mem, w_mem_kv, g_km, g_qm, w_up, conv_w, conv_b, w_down):
    B, S, D = x.shape
    depth = g_mix.shape[0]
    assert S % MOBA_BLOCK == 0 and S // MOBA_BLOCK <= HEAD_DIM
    km, vm = _mem_kv(mem, g_mem, w_mem_kv, g_km)
    h = x
    for i in range(depth):
        j = i // 2
        if i % 2 == 0:
            h = _mixer_a(h, g_mix[i], w_in_a[j], g_sgu[j], w_s[j], b_s[j], km, vm, g_qm[i],
                         w_out_a[j], tm=512)
        else:
            q_hi, q_lo, k_aug, v, kmean, y_mem = _inproj_b(
                h, g_mix[i], w_in_b[j], g_q_b[j], g_k_b[j], km, vm, g_qm[i], tm=512)
            kmp = jnp.pad(kmean.transpose(0, 2, 1, 3),
                          ((0, 0), (0, 0), (HEAD_DIM, HEAD_DIM - kmean.shape[1]), (0, 0)))
            y_attn = _moba(q_hi, q_lo, k_aug, v, kmp, tq=MOBA_BLOCK)
            h = _outproj(h, y_attn, y_mem, w_out_b[j], tm=512)
        h = _ffn(h, g_ffn[i], w_up[i], conv_w[i], conv_b[i], w_down[i], tm=512, ch=256)
    return h
```

```python
import functools
import math

import jax
import jax.numpy as jnp
from jax import lax
from jax.experimental import pallas as pl
from jax.experimental.pallas import tpu as pltpu

F32 = jnp.float32
BF16 = jnp.bfloat16

HEAD_DIM = 64
LANES = 128
N_MEM_HEADS = 4
MEM_WIDTH = N_MEM_HEADS * HEAD_DIM
CHUNK = 128
GROUP_DIM = 128
MOBA_BLOCK = 256
MOBA_TOPK = 3
CONV_WIDTH = 3
HALO = 8
EPS = 1e-6
NEG = -1e30
VMEM_LIMIT = 56 * 1024 * 1024


def _nt(a, b):
    return lax.dot_general(a, b, (((1,), (1,)), ((), ())), preferred_element_type=F32)


def _dot(a, b):
    return jnp.dot(a, b, preferred_element_type=F32)


def _gelu(x):
    c = math.sqrt(2.0 / math.pi)
    return 0.5 * x * (1.0 + jnp.tanh(c * (x + 0.044715 * (x * x * x))))


def _rms(x, g):
    return x * lax.rsqrt(jnp.mean(x * x, axis=-1, keepdims=True) + EPS) * g


def _pair_rms(x):
    lane = lax.broadcasted_iota(jnp.int32, (1, LANES), 1)
    lo = lane < HEAD_DIM
    outs = []
    for c in range(x.shape[1] // LANES):
        xc = x[:, c * LANES:(c + 1) * LANES]
        x2 = xc * xc
        s_lo = jnp.sum(jnp.where(lo, x2, 0.0), axis=-1, keepdims=True)
        s_hi = jnp.sum(jnp.where(lo, 0.0, x2), axis=-1, keepdims=True)
        ms = jnp.where(lo, s_lo, s_hi) * (1.0 / HEAD_DIM)
        outs.append(xc * lax.rsqrt(ms + EPS))
    return jnp.concatenate(outs, axis=-1)


def _spread_heads(x):
    lane = lax.broadcasted_iota(jnp.int32, (1, LANES), 1)
    lo = lane < HEAD_DIM
    outs = []
    for c in range(x.shape[1] // LANES):
        xc = x[:, c * LANES:(c + 1) * LANES]
        outs.append(jnp.where(lo, xc, 0.0))
        outs.append(jnp.where(lo, pltpu.roll(xc, HEAD_DIM, 1), 0.0))
    return jnp.concatenate(outs, axis=-1)


def _mem_heads(qm, gq, km_ref, vm_ref):
    q = (_pair_rms(qm) * (gq * HEAD_DIM ** -0.5)).astype(BF16)
    outs = []
    for hh in range(N_MEM_HEADS):
        ls = slice(hh * HEAD_DIM, (hh + 1) * HEAD_DIM)
        s = _nt(q[:, ls], km_ref[0, :, ls])
        e = jnp.exp(s - jnp.max(s, axis=-1, keepdims=True))
        l = jnp.sum(e, axis=-1, keepdims=True)
        outs.append(_dot(e.astype(BF16), vm_ref[0, :, ls]) / l)
    return jnp.concatenate(outs, axis=-1)


def _mem_kv_kernel(mem_ref, g_ref, w_ref, gk_ref, km_ref, vm_ref):
    xn = _rms(mem_ref[0], g_ref[...]).astype(BF16)
    kv = _dot(xn, w_ref[...])
    km_ref[0] = (_pair_rms(kv[:, :MEM_WIDTH]) * gk_ref[...]).astype(BF16)
    vm_ref[0] = kv[:, MEM_WIDTH:].astype(BF16)


def _mem_kv(mem, g_mem, w_mem_kv, g_km):
    B, M, D = mem.shape
    gk = jnp.tile(g_km, N_MEM_HEADS)[None]
    full = lambda shape: pl.BlockSpec(shape, lambda b: (0,) * len(shape))
    return pl.pallas_call(
        _mem_kv_kernel,
        grid=(B,),
        in_specs=[pl.BlockSpec((1, M, D), lambda b: (b, 0, 0)),
                  full((1, D)), full((D, 2 * MEM_WIDTH)), full((1, MEM_WIDTH))],
        out_specs=[pl.BlockSpec((1, M, MEM_WIDTH), lambda b: (b, 0, 0))] * 2,
        out_shape=[jax.ShapeDtypeStruct((B, M, MEM_WIDTH), BF16)] * 2,
        name="mem_kv",
    )(mem, g_mem[None], w_mem_kv.astype(BF16), gk)


def _mixer_a_kernel(h_ref, g_ref, win_ref, gsgu_ref, ws_ref, bias_ref, km_ref, vm_ref,
                    gqm_ref, wout_ref, o_ref, ycat_ref, *, mixer_width):
    x = h_ref[0]
    tm = x.shape[0]
    n_groups = mixer_width // GROUP_DIM
    n_chunks = tm // CHUNK
    xn = _rms(x, g_ref[...]).astype(BF16)
    p = _dot(xn, win_ref[...])
    z = _gelu(p[:, :2 * mixer_width])
    u = z[:, :mixer_width]
    vn = _rms(z[:, mixer_width:], gsgu_ref[...]).astype(BF16)
    row = lax.broadcasted_iota(jnp.int32, (CHUNK, CHUNK), 0)
    col = lax.broadcasted_iota(jnp.int32, (CHUNK, CHUNK), 1)
    causal = row >= col
    for g in range(n_groups):
        gs = slice(g * GROUP_DIM, (g + 1) * GROUP_DIM)
        wc = jnp.where(causal, ws_ref[g], 0.0).astype(BF16)
        vg = jnp.concatenate([vn[c * CHUNK:(c + 1) * CHUNK, gs] for c in range(n_chunks)], axis=1)
        mixed = _dot(wc, vg)
        for c in range(n_chunks):
            rs = slice(c * CHUNK, (c + 1) * CHUNK)
            m_c = mixed[:, c * GROUP_DIM:(c + 1) * GROUP_DIM] + bias_ref[:, gs]
            ycat_ref[rs, gs] = (u[rs, gs] * m_c).astype(BF16)
    y_mem = _mem_heads(p[:, 2 * mixer_width:], gqm_ref[...], km_ref, vm_ref)
    ycat_ref[:, mixer_width:] = y_mem.astype(BF16)
    o_ref[0] = x + _dot(ycat_ref[...], wout_ref[...])


def _mixer_a(h, g_mix, w_in, g_sgu, w_s, b_s, km, vm, g_qm, w_out, *, tm):
    B, S, D = h.shape
    mixer_width = g_sgu.shape[0]
    n_in = w_in.shape[1]
    M = km.shape[1]
    bias = jnp.repeat(b_s.T, GROUP_DIM, axis=1)
    gqm = jnp.tile(g_qm, N_MEM_HEADS)[None]
    full = lambda shape: pl.BlockSpec(shape, lambda b, i: (0,) * len(shape))
    return pl.pallas_call(
        functools.partial(_mixer_a_kernel, mixer_width=mixer_width),
        grid=(B, S // tm),
        in_specs=[pl.BlockSpec((1, tm, D), lambda b, i: (b, i, 0)),
                  full((1, D)), full((D, n_in)), full((1, mixer_width)),
                  full(w_s.shape), full((CHUNK, mixer_width)),
                  pl.BlockSpec((1, M, MEM_WIDTH), lambda b, i: (b, 0, 0)),
                  pl.BlockSpec((1, M, MEM_WIDTH), lambda b, i: (b, 0, 0)),
                  full((1, MEM_WIDTH)), full((D, D))],
        out_specs=pl.BlockSpec((1, tm, D), lambda b, i: (b, i, 0)),
        out_shape=jax.ShapeDtypeStruct((B, S, D), F32),
        scratch_shapes=[pltpu.VMEM((tm, D), BF16)],
        compiler_params=pltpu.CompilerParams(
            dimension_semantics=("parallel", "parallel"), vmem_limit_bytes=VMEM_LIMIT),
        name="mixer_a",
    )(h, g_mix[None], w_in.astype(BF16), g_sgu[None], w_s, bias, km, vm, gqm, w_out.astype(BF16))


def _ffn_kernel(h_ref, halo_ref, g_ref, wup_ref, cw_ref, cb_ref, wdn_ref, o_ref, a_ref,
                *, d_ff, ch):
    i = pl.program_id(1)
    x = h_ref[0]
    tm = x.shape[0]
    g = g_ref[...]
    xn = _rms(x, g).astype(BF16)
    xh = _rms(halo_ref[0], g)
    xh = jnp.where(i > 0, xh, 0.0).astype(BF16)
    xe = jnp.concatenate([xh, xn], axis=0)

    def conv_part(cols):
        hh = _dot(xe, wup_ref[:, cols])
        w = cw_ref[:, cols]
        return (w[2:3] * hh[HALO:] + w[1:2] * hh[HALO - 1:HALO - 1 + tm]
                + w[0:1] * hh[HALO - 2:HALO - 2 + tm] + cb_ref[:, cols])

    for c in range(d_ff // ch):
        gate = conv_part(slice(c * ch, (c + 1) * ch))
        val = conv_part(slice(d_ff + c * ch, d_ff + (c + 1) * ch))
        a_ref[:, c * ch:(c + 1) * ch] = (_gelu(gate) * val).astype(BF16)
    o_ref[0] = x + _dot(a_ref[...], wdn_ref[...])


def _ffn(h, g_ffn, w_up, conv_w, conv_b, w_down, *, tm, ch):
    B, S, D = h.shape
    d_ff = w_down.shape[0]
    full = lambda shape: pl.BlockSpec(shape, lambda b, i: (0,) * len(shape))
    hb = tm // HALO
    return pl.pallas_call(
        functools.partial(_ffn_kernel, d_ff=d_ff, ch=ch),
        grid=(B, S // tm),
        in_specs=[pl.BlockSpec((1, tm, D), lambda b, i: (b, i, 0)),
                  pl.BlockSpec((1, HALO, D), lambda b, i: (b, jnp.maximum(i * hb - 1, 0), 0)),
                  full((1, D)), full((D, 2 * d_ff)), full((CONV_WIDTH, 2 * d_ff)),
                  full((1, 2 * d_ff)), full((d_ff, D))],
        out_specs=pl.BlockSpec((1, tm, D), lambda b, i: (b, i, 0)),
        out_shape=jax.ShapeDtypeStruct((B, S, D), F32),
        scratch_shapes=[pltpu.VMEM((tm, d_ff), BF16)],
        compiler_params=pltpu.CompilerParams(
            dimension_semantics=("parallel", "parallel"), vmem_limit_bytes=VMEM_LIMIT),
        name="ffn",
    )(h, h, g_ffn[None], w_up.astype(BF16), conv_w, conv_b[None], w_down.astype(BF16))


def _inproj_b_kernel(h_ref, g_ref, win_ref, gq_ref, gk_ref, km_ref, vm_ref, gqm_ref,
                     qhi_ref, qlo_ref, k_ref, v_ref, kmean_ref, ymem_ref, *, mixer_width):
    i = pl.program_id(1)
    x = h_ref[0]
    tm = x.shape[0]
    xn = _rms(x, g_ref[...]).astype(BF16)
    p = _dot(xn, win_ref[...])
    q = _spread_heads(p[:, :mixer_width])
    k = _spread_heads(p[:, mixer_width:2 * mixer_width])
    n_heads = mixer_width // HEAD_DIM

    def head_norm(a, gain):
        outs = []
        for h in range(n_heads):
            ah = a[:, h * LANES:(h + 1) * LANES]
            ms = jnp.sum(ah * ah, axis=-1, keepdims=True) * (1.0 / HEAD_DIM)
            outs.append(ah * lax.rsqrt(ms + EPS))
        return jnp.concatenate(outs, axis=-1) * gain

    q = head_norm(q, gq_ref[...] * HEAD_DIM ** -0.5)
    k = head_norm(k, gk_ref[...])
    q_hi = q.astype(BF16)
    qhi_ref[0] = q_hi
    qlo_ref[0] = (q - q_hi.astype(F32)).astype(BF16)
    nb = tm // MOBA_BLOCK
    kmean_ref[0] = jnp.concatenate(
        [jnp.mean(k[r * MOBA_BLOCK:(r + 1) * MOBA_BLOCK], axis=0, keepdims=True) for r in range(nb)],
        axis=0)
    lane = lax.broadcasted_iota(jnp.int32, (tm, LANES), 1)
    blk = (i * tm + lax.broadcasted_iota(jnp.int32, (tm, LANES), 0)) // MOBA_BLOCK
    e = jnp.where(lane - HEAD_DIM == blk, NEG, 0.0)
    lo = lane < HEAD_DIM
    k_ref[0] = jnp.concatenate(
        [jnp.where(lo, k[:, h * LANES:(h + 1) * LANES], e) for h in range(n_heads)],
        axis=-1).astype(BF16)
    v = _spread_heads(p[:, 2 * mixer_width:3 * mixer_width])
    ones_lane = jnp.where(lane == HEAD_DIM, 1.0, 0.0)
    v_ref[0] = jnp.concatenate(
        [v[:, h * LANES:(h + 1) * LANES] + ones_lane for h in range(n_heads)], axis=-1).astype(BF16)
    ymem_ref[0] = _mem_heads(p[:, 3 * mixer_width:], gqm_ref[...], km_ref, vm_ref).astype(BF16)


def _pad_gain(g, n_heads):
    return jnp.tile(jnp.concatenate([g, jnp.zeros_like(g)]), n_heads)[None]


def _inproj_b(h, g_mix, w_in, g_q, g_k, km, vm, g_qm, *, tm):
    B, S, D = h.shape
    n_in = w_in.shape[1]
    mixer_width = (n_in - MEM_WIDTH) // 3
    n_heads = mixer_width // HEAD_DIM
    wide = n_heads * LANES
    M = km.shape[1]
    nt = S // tm
    nb = tm // MOBA_BLOCK
    gqm = jnp.tile(g_qm, N_MEM_HEADS)[None]
    full = lambda shape: pl.BlockSpec(shape, lambda b, i: (0,) * len(shape))
    tile = lambda w: pl.BlockSpec((1, tm, w), lambda b, i: (b, i, 0))
    outs = pl.pallas_call(
        functools.partial(_inproj_b_kernel, mixer_width=mixer_width),
        grid=(B, nt),
        in_specs=[tile(D), full((1, D)), full((D, n_in)), full((1, wide)), full((1, wide)),
                  pl.BlockSpec((1, M, MEM_WIDTH), lambda b, i: (b, 0, 0)),
                  pl.BlockSpec((1, M, MEM_WIDTH), lambda b, i: (b, 0, 0)),
                  full((1, MEM_WIDTH))],
        out_specs=[tile(wide), tile(wide), tile(wide), tile(wide),
                   pl.BlockSpec((1, nb, wide), lambda b, i: (b * nt + i, 0, 0)),
                   tile(MEM_WIDTH)],
        out_shape=[jax.ShapeDtypeStruct((B, S, wide), BF16),
                   jax.ShapeDtypeStruct((B, S, wide), BF16),
                   jax.ShapeDtypeStruct((B, S, wide), BF16),
                   jax.ShapeDtypeStruct((B, S, wide), BF16),
                   jax.ShapeDtypeStruct((B * nt, nb, wide), F32),
                   jax.ShapeDtypeStruct((B, S, MEM_WIDTH), BF16)],
        compiler_params=pltpu.CompilerParams(
            dimension_semantics=("parallel", "parallel"), vmem_limit_bytes=VMEM_LIMIT),
        name="inproj_b",
    )(h, g_mix[None], w_in.astype(BF16), _pad_gain(g_q, n_heads), _pad_gain(g_k, n_heads),
      km, vm, gqm)
    q_hi, q_lo, k_aug, v, kmean, y_mem = outs
    return q_hi, q_lo, k_aug, v, kmean.reshape(B, S // MOBA_BLOCK, n_heads, LANES), y_mem


def _moba_kernel(qhi_ref, qlo_ref, k_ref, v_ref, kmp_ref, o_ref, *, group):
    own = pl.program_id(2)
    tq = qhi_ref.shape[1]
    span = group * MOBA_BLOCK
    lane = lax.broadcasted_iota(jnp.int32, (1, LANES), 1)
    blk = lane - HEAD_DIM
    row = lax.broadcasted_iota(jnp.int32, (tq, MOBA_BLOCK), 0)
    col = lax.broadcasted_iota(jnp.int32, (tq, MOBA_BLOCK), 1)
    causal = col <= row
    own_start = pl.multiple_of(own * MOBA_BLOCK, MOBA_BLOCK)
    q_aug, init = [], []
    for hd in range(2):
        ls = slice(hd * LANES, (hd + 1) * LANES)
        q_hi = qhi_ref[0, :, ls]
        q_lo = qlo_ref[0, :, ls]
        kmp = kmp_ref[0, hd]
        kmp_hi = kmp.astype(BF16)
        kmp_lo = (kmp - kmp_hi.astype(F32)).astype(BF16)
        gate = _nt(q_hi, kmp_hi) + (_nt(q_lo, kmp_hi) + _nt(q_hi, kmp_lo))
        gate = jnp.where(blk < 0, -jnp.inf, jnp.where(blk < own, gate, NEG))
        sel = jnp.zeros(gate.shape, jnp.bool_)
        for j in range(MOBA_TOPK):
            mx = jnp.max(gate, axis=-1, keepdims=True)
            first = jnp.min(jnp.where(gate == mx, lane, LANES), axis=-1, keepdims=True)
            pick = lane == first
            sel = jnp.logical_or(sel, jnp.logical_and(pick, j < own))
            gate = jnp.where(pick, -jnp.inf, gate)
        not_sel = jnp.where(jnp.logical_or(sel, blk < 0), 0.0, 1.0).astype(BF16)
        q_aug.append(q_hi + not_sel)
        s = jnp.where(causal, _nt(q_hi, k_ref[0, pl.ds(own_start, MOBA_BLOCK), ls]), NEG)
        m0 = jnp.max(s, axis=-1, keepdims=True)
        p = jnp.exp(s - m0).astype(BF16)
        init.append((m0, _dot(p, v_ref[0, pl.ds(own_start, MOBA_BLOCK), ls])))

    def body(g, carry):
        start = pl.multiple_of(g * span, span)
        new = []
        for hd in range(2):
            ls = slice(hd * LANES, (hd + 1) * LANES)
            m, acc = carry[hd]
            s = _nt(q_aug[hd], k_ref[0, pl.ds(start, span), ls])
            m_new = jnp.maximum(m, jnp.max(s, axis=-1, keepdims=True))
            p = jnp.exp(s - m_new).astype(BF16)
            acc = jnp.exp(m - m_new) * acc + _dot(p, v_ref[0, pl.ds(start, span), ls])
            new.append((m_new, acc))
        return tuple(new)

    final = lax.fori_loop(0, (own + group - 1) // group, body, tuple(init))
    outs = [acc / acc[:, HEAD_DIM:HEAD_DIM + 1] for _, acc in final]
    o_ref[0] = jnp.where(lane < HEAD_DIM, outs[0], pltpu.roll(outs[1], HEAD_DIM, 1)).astype(o_ref.dtype)


def _moba(q_hi, q_lo, k_aug, v_aug, kmp, *, tq, group):
    B, S, wide = q_hi.shape
    n_pairs = wide // (2 * LANES)
    assert (S // MOBA_BLOCK) % group == 0
    pair_tile = pl.BlockSpec((1, tq, 2 * LANES), lambda b, hp, t: (b, t, hp))
    pair_seq = pl.BlockSpec((1, S, 2 * LANES), lambda b, hp, t: (b, 0, hp))
    return pl.pallas_call(
        functools.partial(_moba_kernel, group=group),
        grid=(B, n_pairs, S // tq),
        in_specs=[pair_tile, pair_tile, pair_seq, pair_seq,
                  pl.BlockSpec((1, 2, LANES, LANES), lambda b, hp, t: (b, hp, 0, 0))],
        out_specs=pl.BlockSpec((1, tq, LANES), lambda b, hp, t: (b, t, hp)),
        out_shape=jax.ShapeDtypeStruct((B, S, n_pairs * LANES), BF16),
        compiler_params=pltpu.CompilerParams(
            dimension_semantics=("parallel", "parallel", "arbitrary"),
            vmem_limit_bytes=VMEM_LIMIT),
        name="moba",
    )(q_hi, q_lo, k_aug, v_aug, kmp)


def _outproj_kernel(h_ref, ya_ref, ym_ref, w_ref, o_ref):
    wa = ya_ref.shape[2]
    y = _dot(ya_ref[0], w_ref[:wa, :]) + _dot(ym_ref[0], w_ref[wa:, :])
    o_ref[0] = h_ref[0] + y


def _outproj(h, y_attn, y_mem, w_out, *, tm):
    B, S, D = h.shape
    tile = lambda w: pl.BlockSpec((1, tm, w), lambda b, i: (b, i, 0))
    return pl.pallas_call(
        _outproj_kernel,
        grid=(B, S // tm),
        in_specs=[tile(D), tile(y_attn.shape[2]), tile(y_mem.shape[2]),
                  pl.BlockSpec(w_out.shape, lambda b, i: (0, 0))],
        out_specs=tile(D),
        out_shape=jax.ShapeDtypeStruct((B, S, D), F32),
        compiler_params=pltpu.CompilerParams(
            dimension_semantics=("parallel", "parallel"), vmem_limit_bytes=VMEM_LIMIT),
        name="outproj_b",
    )(h, y_attn, y_mem, w_out.astype(BF16))


def kernel(x, mem, g_mix, g_ffn, w_in_a, w_out_a, g_sgu, w_s, b_s, w_in_b, w_out_b, g_q_b,
           g_k_b, g_mem, w_mem_kv, g_km, g_qm, w_up, conv_w, conv_b, w_down):
    B, S, D = x.shape
    depth = g_mix.shape[0]
    assert S % MOBA_BLOCK == 0 and S // MOBA_BLOCK <= HEAD_DIM
    km, vm = _mem_kv(mem, g_mem, w_mem_kv, g_km)
    h = x
    for i in range(depth):
        j = i // 2
        if i % 2 == 0:
            h = _mixer_a(h, g_mix[i], w_in_a[j], g_sgu[j], w_s[j], b_s[j], km, vm, g_qm[i],
                         w_out_a[j], tm=512)
        else:
            q_hi, q_lo, k_aug, v, kmean, y_mem = _inproj_b(
                h, g_mix[i], w_in_b[j], g_q_b[j], g_k_b[j], km, vm, g_qm[i], tm=512)
            kmp = jnp.pad(kmean.transpose(0, 2, 1, 3),
                          ((0, 0), (0, 0), (HEAD_DIM, HEAD_DIM - kmean.shape[1]), (0, 0)))
            y_attn = _moba(q_hi, q_lo, k_aug, v, kmp, tq=MOBA_BLOCK, group=4)
            h = _outproj(h, y_attn, y_mem, w_out_b[j], tm=512)
        h = _ffn(h, g_ffn[i], w_up[i], conv_w[i], conv_b[i], w_down[i], tm=512, ch=256)
    return h
```

```python
import functools
import math

import jax
import jax.numpy as jnp
from jax import lax
from jax.experimental import pallas as pl
from jax.experimental.pallas import tpu as pltpu

F32 = jnp.float32
BF16 = jnp.bfloat16

HEAD_DIM = 64
LANES = 128
BF16_ROWS = 16
N_MEM_HEADS = 4
MEM_WIDTH = N_MEM_HEADS * HEAD_DIM
CHUNK = 128
GROUP_DIM = 128
MOBA_BLOCK = 256
MOBA_TOPK = 3
V_ROWS = HEAD_DIM + BF16_ROWS
CONV_WIDTH = 3
HALO = 8
EPS = 1e-6
NEG = -1e30
VMEM_LIMIT = 56 * 1024 * 1024


def _nt(a, b):
    return lax.dot_general(a, b, (((1,), (1,)), ((), ())), preferred_element_type=F32)


def _dot(a, b):
    return jnp.dot(a, b, preferred_element_type=F32)


def _gelu(x):
    c = math.sqrt(2.0 / math.pi)
    return 0.5 * x * (1.0 + jnp.tanh(c * (x + 0.044715 * (x * x * x))))


def _rms(x, g):
    return x * lax.rsqrt(jnp.mean(x * x, axis=-1, keepdims=True) + EPS) * g


def _pair_rms(x):
    lane = lax.broadcasted_iota(jnp.int32, (1, LANES), 1)
    lo = lane < HEAD_DIM
    outs = []
    for c in range(x.shape[1] // LANES):
        xc = x[:, c * LANES:(c + 1) * LANES]
        x2 = xc * xc
        s_lo = jnp.sum(jnp.where(lo, x2, 0.0), axis=-1, keepdims=True)
        s_hi = jnp.sum(jnp.where(lo, 0.0, x2), axis=-1, keepdims=True)
        ms = jnp.where(lo, s_lo, s_hi) * (1.0 / HEAD_DIM)
        outs.append(xc * lax.rsqrt(ms + EPS))
    return jnp.concatenate(outs, axis=-1)


def _mem_heads(qm, gq, km_ref, vm_ref):
    q = (_pair_rms(qm) * (gq * HEAD_DIM ** -0.5)).astype(BF16)
    outs = []
    for hh in range(N_MEM_HEADS):
        ls = slice(hh * HEAD_DIM, (hh + 1) * HEAD_DIM)
        s = _nt(q[:, ls], km_ref[0, :, ls])
        e = jnp.exp(s - jnp.max(s, axis=-1, keepdims=True))
        l = jnp.sum(e, axis=-1, keepdims=True)
        outs.append(_dot(e.astype(BF16), vm_ref[0, :, ls]) / l)
    return jnp.concatenate(outs, axis=-1)


def _mem_kv_kernel(mem_ref, g_ref, w_ref, gk_ref, km_ref, vm_ref):
    xn = _rms(mem_ref[0], g_ref[...]).astype(BF16)
    kv = _dot(xn, w_ref[...])
    km_ref[0] = (_pair_rms(kv[:, :MEM_WIDTH]) * gk_ref[...]).astype(BF16)
    vm_ref[0] = kv[:, MEM_WIDTH:].astype(BF16)


def _mem_kv(mem, g_mem, w_mem_kv, g_km):
    B, M, D = mem.shape
    gk = jnp.tile(g_km, N_MEM_HEADS)[None]
    full = lambda shape: pl.BlockSpec(shape, lambda b: (0,) * len(shape))
    return pl.pallas_call(
        _mem_kv_kernel,
        grid=(B,),
        in_specs=[pl.BlockSpec((1, M, D), lambda b: (b, 0, 0)),
                  full((1, D)), full((D, 2 * MEM_WIDTH)), full((1, MEM_WIDTH))],
        out_specs=[pl.BlockSpec((1, M, MEM_WIDTH), lambda b: (b, 0, 0))] * 2,
        out_shape=[jax.ShapeDtypeStruct((B, M, MEM_WIDTH), BF16)] * 2,
        name="mem_kv",
    )(mem, g_mem[None], w_mem_kv.astype(BF16), gk)


def _mixer_a_kernel(h_ref, g_ref, win_ref, gsgu_ref, ws_ref, bias_ref, km_ref, vm_ref,
                    gqm_ref, wout_ref, o_ref, ycat_ref, *, mixer_width):
    x = h_ref[0]
    tm = x.shape[0]
    n_groups = mixer_width // GROUP_DIM
    n_chunks = tm // CHUNK
    xn = _rms(x, g_ref[...]).astype(BF16)
    p = _dot(xn, win_ref[...])
    z = _gelu(p[:, :2 * mixer_width])
    u = z[:, :mixer_width]
    vn = _rms(z[:, mixer_width:], gsgu_ref[...]).astype(BF16)
    row = lax.broadcasted_iota(jnp.int32, (CHUNK, CHUNK), 0)
    col = lax.broadcasted_iota(jnp.int32, (CHUNK, CHUNK), 1)
    causal = row >= col
    for g in range(n_groups):
        gs = slice(g * GROUP_DIM, (g + 1) * GROUP_DIM)
        wc = jnp.where(causal, ws_ref[g], 0.0).astype(BF16)
        vg = jnp.concatenate([vn[c * CHUNK:(c + 1) * CHUNK, gs] for c in range(n_chunks)], axis=1)
        mixed = _dot(wc, vg)
        for c in range(n_chunks):
            rs = slice(c * CHUNK, (c + 1) * CHUNK)
            m_c = mixed[:, c * GROUP_DIM:(c + 1) * GROUP_DIM] + bias_ref[:, gs]
            ycat_ref[rs, gs] = (u[rs, gs] * m_c).astype(BF16)
    y_mem = _mem_heads(p[:, 2 * mixer_width:], gqm_ref[...], km_ref, vm_ref)
    ycat_ref[:, mixer_width:] = y_mem.astype(BF16)
    o_ref[0] = x + _dot(ycat_ref[...], wout_ref[...])


def _mixer_a(h, g_mix, w_in, g_sgu, w_s, b_s, km, vm, g_qm, w_out, *, tm):
    B, S, D = h.shape
    mixer_width = g_sgu.shape[0]
    n_in = w_in.shape[1]
    M = km.shape[1]
    bias = jnp.repeat(b_s.T, GROUP_DIM, axis=1)
    gqm = jnp.tile(g_qm, N_MEM_HEADS)[None]
    full = lambda shape: pl.BlockSpec(shape, lambda b, i: (0,) * len(shape))
    return pl.pallas_call(
        functools.partial(_mixer_a_kernel, mixer_width=mixer_width),
        grid=(B, S // tm),
        in_specs=[pl.BlockSpec((1, tm, D), lambda b, i: (b, i, 0)),
                  full((1, D)), full((D, n_in)), full((1, mixer_width)),
                  full(w_s.shape), full((CHUNK, mixer_width)),
                  pl.BlockSpec((1, M, MEM_WIDTH), lambda b, i: (b, 0, 0)),
                  pl.BlockSpec((1, M, MEM_WIDTH), lambda b, i: (b, 0, 0)),
                  full((1, MEM_WIDTH)), full((D, D))],
        out_specs=pl.BlockSpec((1, tm, D), lambda b, i: (b, i, 0)),
        out_shape=jax.ShapeDtypeStruct((B, S, D), F32),
        scratch_shapes=[pltpu.VMEM((tm, D), BF16)],
        compiler_params=pltpu.CompilerParams(
            dimension_semantics=("parallel", "parallel"), vmem_limit_bytes=VMEM_LIMIT),
        name="mixer_a",
    )(h, g_mix[None], w_in.astype(BF16), g_sgu[None], w_s, bias, km, vm, gqm, w_out.astype(BF16))


def _ffn_kernel(h_ref, halo_ref, g_ref, wup_ref, cw_ref, cb_ref, wdn_ref, o_ref, a_ref,
                *, d_ff, ch):
    i = pl.program_id(1)
    x = h_ref[0]
    tm = x.shape[0]
    g = g_ref[...]
    xn = _rms(x, g).astype(BF16)
    xh = _rms(halo_ref[0], g)
    xh = jnp.where(i > 0, xh, 0.0).astype(BF16)
    xe = jnp.concatenate([xh, xn], axis=0)

    def conv_part(cols):
        hh = _dot(xe, wup_ref[:, cols])
        w = cw_ref[:, cols]
        return (w[2:3] * hh[HALO:] + w[1:2] * hh[HALO - 1:HALO - 1 + tm]
                + w[0:1] * hh[HALO - 2:HALO - 2 + tm] + cb_ref[:, cols])

    for c in range(d_ff // ch):
        gate = conv_part(slice(c * ch, (c + 1) * ch))
        val = conv_part(slice(d_ff + c * ch, d_ff + (c + 1) * ch))
        a_ref[:, c * ch:(c + 1) * ch] = (_gelu(gate) * val).astype(BF16)
    o_ref[0] = x + _dot(a_ref[...], wdn_ref[...])


def _ffn(h, g_ffn, w_up, conv_w, conv_b, w_down, *, tm, ch):
    B, S, D = h.shape
    d_ff = w_down.shape[0]
    full = lambda shape: pl.BlockSpec(shape, lambda b, i: (0,) * len(shape))
    hb = tm // HALO
    return pl.pallas_call(
        functools.partial(_ffn_kernel, d_ff=d_ff, ch=ch),
        grid=(B, S // tm),
        in_specs=[pl.BlockSpec((1, tm, D), lambda b, i: (b, i, 0)),
                  pl.BlockSpec((1, HALO, D), lambda b, i: (b, jnp.maximum(i * hb - 1, 0), 0)),
                  full((1, D)), full((D, 2 * d_ff)), full((CONV_WIDTH, 2 * d_ff)),
                  full((1, 2 * d_ff)), full((d_ff, D))],
        out_specs=pl.BlockSpec((1, tm, D), lambda b, i: (b, i, 0)),
        out_shape=jax.ShapeDtypeStruct((B, S, D), F32),
        scratch_shapes=[pltpu.VMEM((tm, d_ff), BF16)],
        compiler_params=pltpu.CompilerParams(
            dimension_semantics=("parallel", "parallel"), vmem_limit_bytes=VMEM_LIMIT),
        name="ffn",
    )(h, h, g_ffn[None], w_up.astype(BF16), conv_w, conv_b[None], w_down.astype(BF16))


def _inproj_b_kernel(h_ref, g_ref, wqk_ref, wvt_ref, wmem_ref, gq_ref, gk_ref, km_ref, vm_ref,
                     gqm_ref, qhi_ref, qlo_ref, k_ref, vt_ref, kmean_ref, ymem_ref):
    x = h_ref[0]
    tm = x.shape[0]
    xn = _rms(x, g_ref[...]).astype(BF16)
    width = gq_ref.shape[1]
    n_heads = width // HEAD_DIM
    qk = _dot(xn, wqk_ref[...])
    q = _pair_rms(qk[:, :width]) * (gq_ref[...] * HEAD_DIM ** -0.5)
    k = _pair_rms(qk[:, width:]) * gk_ref[...]
    q_hi = q.astype(BF16)
    qhi_ref[0] = q_hi
    qlo_ref[0] = (q - q_hi.astype(F32)).astype(BF16)
    k_ref[0] = k.astype(BF16)
    nb = tm // MOBA_BLOCK
    kmean_ref[0] = jnp.concatenate(
        [jnp.mean(k[r * MOBA_BLOCK:(r + 1) * MOBA_BLOCK], axis=0, keepdims=True) for r in range(nb)],
        axis=0)
    vt = _nt(wvt_ref[...], xn)
    tail = jnp.where(lax.broadcasted_iota(jnp.int32, (BF16_ROWS, tm), 0) == 0, 1.0, 0.0)
    pieces = []
    for h in range(n_heads):
        pieces += [vt[h * HEAD_DIM:(h + 1) * HEAD_DIM], tail]
    vt = jnp.concatenate(pieces, axis=0).astype(BF16)
    for r in range(nb):
        vt_ref[0, r] = vt[:, r * MOBA_BLOCK:(r + 1) * MOBA_BLOCK]
    ymem_ref[0] = _mem_heads(_dot(xn, wmem_ref[...]), gqm_ref[...], km_ref, vm_ref).astype(BF16)


def _inproj_b(h, g_mix, w_in, g_q, g_k, km, vm, g_qm, *, tm):
    B, S, D = h.shape
    width = (w_in.shape[1] - MEM_WIDTH) // 3
    n_heads = width // HEAD_DIM
    M = km.shape[1]
    nt = S // tm
    nb = tm // MOBA_BLOCK
    w_qk = w_in[:, :2 * width].astype(BF16)
    w_vt = w_in[:, 2 * width:3 * width].T.astype(BF16)
    w_mem = w_in[:, 3 * width:].astype(BF16)
    gq = jnp.tile(g_q, n_heads)[None]
    gk = jnp.tile(g_k, n_heads)[None]
    gqm = jnp.tile(g_qm, N_MEM_HEADS)[None]
    full = lambda shape: pl.BlockSpec(shape, lambda b, i: (0,) * len(shape))
    tile = lambda w: pl.BlockSpec((1, tm, w), lambda b, i: (b, i, 0))
    outs = pl.pallas_call(
        _inproj_b_kernel,
        grid=(B, nt),
        in_specs=[tile(D), full((1, D)), full(w_qk.shape), full(w_vt.shape), full(w_mem.shape),
                  full((1, width)), full((1, width)),
                  pl.BlockSpec((1, M, MEM_WIDTH), lambda b, i: (b, 0, 0)),
                  pl.BlockSpec((1, M, MEM_WIDTH), lambda b, i: (b, 0, 0)),
                  full((1, MEM_WIDTH))],
        out_specs=[tile(width), tile(width), tile(width),
                   pl.BlockSpec((1, nb, n_heads * V_ROWS, MOBA_BLOCK), lambda b, i: (b, i, 0, 0)),
                   pl.BlockSpec((1, nb, width), lambda b, i: (b * nt + i, 0, 0)),
                   tile(MEM_WIDTH)],
        out_shape=[jax.ShapeDtypeStruct((B, S, width), BF16),
                   jax.ShapeDtypeStruct((B, S, width), BF16),
                   jax.ShapeDtypeStruct((B, S, width), BF16),
                   jax.ShapeDtypeStruct((B, S // MOBA_BLOCK, n_heads * V_ROWS, MOBA_BLOCK), BF16),
                   jax.ShapeDtypeStruct((B * nt, nb, width), F32),
                   jax.ShapeDtypeStruct((B, S, MEM_WIDTH), BF16)],
        compiler_params=pltpu.CompilerParams(
            dimension_semantics=("parallel", "parallel"), vmem_limit_bytes=VMEM_LIMIT),
        name="inproj_b",
    )(h, g_mix[None], w_qk, w_vt, w_mem, gq, gk, km, vm, gqm)
    q_hi, q_lo, k, vt, kmean, y_mem = outs
    return q_hi, q_lo, k, vt, kmean.reshape(B, S // MOBA_BLOCK, width), y_mem


def _moba_kernel(qhi_ref, qlo_ref, k_ref, vt_ref, kmean_ref, o_ref, bias_ref, s_ref, mx_ref, *,
                 group):
    own = pl.program_id(2)
    tq = qhi_ref.shape[1]
    nb = kmean_ref.shape[1]
    span = group * MOBA_BLOCK
    last_group = nb // group - 1
    rows = [slice(i * MOBA_BLOCK, (i + 1) * MOBA_BLOCK) for i in range(group)]
    cols = [slice(hd * tq, (hd + 1) * tq) for hd in range(2)]
    lane = lax.broadcasted_iota(jnp.int32, (1, LANES), 1)
    zero = jnp.zeros((), BF16)

    def both(x):
        return jnp.concatenate([jnp.where(lane < HEAD_DIM, x, zero), jnp.where(lane >= HEAD_DIM, x, zero)],
                               axis=0)

    q_both, q_lo = both(qhi_ref[0]), both(qlo_ref[0])
    km = kmean_ref[0]
    km_hi = km.astype(BF16)
    km_lo = (km - km_hi.astype(F32)).astype(BF16)

    def scores_into(g, i):
        start = pl.multiple_of((jnp.minimum(g, last_group) * group + i) * MOBA_BLOCK, MOBA_BLOCK)
        s = _nt(k_ref[0, pl.ds(start, MOBA_BLOCK), :], q_both)
        s_ref[rows[i], :] = s
        mx_ref[i:i + 1, :] = jnp.max(s, axis=0, keepdims=True)

    gate = _nt(km_hi, q_both) + (_nt(km_hi, q_lo) + _nt(km_lo, q_both))
    k_own = k_ref[0, pl.ds(pl.multiple_of(own * MOBA_BLOCK, MOBA_BLOCK), MOBA_BLOCK), :]
    s_own = _nt(k_own, q_both)
    for i in range(group):
        scores_into(0, i)

    blk = lax.broadcasted_iota(jnp.int32, (nb, 2 * tq), 0)
    gate = jnp.where(blk < own, gate, NEG)
    sel = jnp.zeros(gate.shape, jnp.bool_)
    for j in range(MOBA_TOPK):
        mx = jnp.max(gate, axis=0, keepdims=True)
        first = jnp.min(jnp.where(gate == mx, blk, nb), axis=0, keepdims=True)
        pick = blk == first
        sel = jnp.logical_or(sel, jnp.logical_and(pick, j < own))
        gate = jnp.where(pick, -jnp.inf, gate)
    bias_ref[...] = jnp.where(sel, 0.0, NEG)

    key = lax.broadcasted_iota(jnp.int32, (MOBA_BLOCK, 2 * tq), 0)
    qry = lax.broadcasted_iota(jnp.int32, (MOBA_BLOCK, 2 * tq), 1)
    s_own = jnp.where(key <= jnp.where(qry < tq, qry, qry - tq), s_own, NEG)
    m0 = jnp.max(s_own, axis=0, keepdims=True)
    p_own = jnp.exp(s_own - m0).astype(BF16)
    init = tuple((m0[:, cols[hd]], _dot(vt_ref[0, own, hd * V_ROWS:(hd + 1) * V_ROWS, :], p_own[:, cols[hd]]))
                 for hd in range(2))

    def body(g, state):
        bias, m_new, acc = [], [], []
        for hd in range(2):
            m, a = state[hd]
            b = [bias_ref[pl.ds(g * group + i, 1), cols[hd]] for i in range(group)]
            mx = m
            for i in range(group):
                mx = jnp.maximum(mx, mx_ref[i:i + 1, cols[hd]] + b[i])
            bias.append(b)
            m_new.append(mx)
            acc.append(jnp.exp(m - mx) * a)
        for i in range(group):
            for hd in range(2):
                p = jnp.exp((s_ref[rows[i], cols[hd]] + (bias[hd][i] - m_new[hd])).astype(BF16))
                acc[hd] = acc[hd] + _dot(vt_ref[0, g * group + i, hd * V_ROWS:(hd + 1) * V_ROWS, :], p)
            scores_into(g + 1, i)
        return tuple(zip(m_new, acc))

    final = lax.fori_loop(0, (own + group - 1) // group, body, init)
    out_t = jnp.concatenate(
        [acc[:HEAD_DIM] / acc[HEAD_DIM:HEAD_DIM + 1] for _, acc in final], axis=0)
    o_ref[0] = out_t.T.astype(o_ref.dtype)


def _moba(q_hi, q_lo, k, vt, kmean, *, tq, group):
    B, S, width = q_hi.shape
    nb = S // MOBA_BLOCK
    n_pairs = width // LANES
    assert nb % group == 0 and tq == MOBA_BLOCK
    pair_tile = pl.BlockSpec((1, tq, LANES), lambda b, hp, t: (b, t, hp))
    return pl.pallas_call(
        functools.partial(_moba_kernel, group=group),
        grid=(B, n_pairs, S // tq),
        in_specs=[pair_tile, pair_tile,
                  pl.BlockSpec((1, S, LANES), lambda b, hp, t: (b, 0, hp)),
                  pl.BlockSpec((1, nb, 2 * V_ROWS, MOBA_BLOCK), lambda b, hp, t: (b, 0, hp, 0)),
                  pl.BlockSpec((1, nb, LANES), lambda b, hp, t: (b, 0, hp))],
        out_specs=pair_tile,
        out_shape=jax.ShapeDtypeStruct((B, S, width), BF16),
        scratch_shapes=[pltpu.VMEM((nb, 2 * tq), F32),
                        pltpu.VMEM((group * MOBA_BLOCK, 2 * tq), F32),
                        pltpu.VMEM((group, 2 * tq), F32)],
        compiler_params=pltpu.CompilerParams(
            dimension_semantics=("parallel", "parallel", "arbitrary"),
            vmem_limit_bytes=VMEM_LIMIT),
        name="moba",
    )(q_hi, q_lo, k, vt, kmean)


def _outproj_kernel(h_ref, ya_ref, ym_ref, w_ref, o_ref):
    wa = ya_ref.shape[2]
    y = _dot(ya_ref[0], w_ref[:wa, :]) + _dot(ym_ref[0], w_ref[wa:, :])
    o_ref[0] = h_ref[0] + y


def _outproj(h, y_attn, y_mem, w_out, *, tm):
    B, S, D = h.shape
    tile = lambda w: pl.BlockSpec((1, tm, w), lambda b, i: (b, i, 0))
    return pl.pallas_call(
        _outproj_kernel,
        grid=(B, S // tm),
        in_specs=[tile(D), tile(y_attn.shape[2]), tile(y_mem.shape[2]),
                  pl.BlockSpec(w_out.shape, lambda b, i: (0, 0))],
        out_specs=tile(D),
        out_shape=jax.ShapeDtypeStruct((B, S, D), F32),
        compiler_params=pltpu.CompilerParams(
            dimension_semantics=("parallel", "parallel"), vmem_limit_bytes=VMEM_LIMIT),
        name="outproj_b",
    )(h, y_attn, y_mem, w_out.astype(BF16))


def kernel(x, mem, g_mix, g_ffn, w_in_a, w_out_a, g_sgu, w_s, b_s, w_in_b, w_out_b, g_q_b,
           g_k_b, g_mem, w_mem_kv, g_km, g_qm, w_up, conv_w, conv_b, w_down):
    B, S, D = x.shape
    depth = g_mix.shape[0]
    assert S % MOBA_BLOCK == 0
    km, vm = _mem_kv(mem, g_mem, w_mem_kv, g_km)
    h = x
    for i in range(depth):
        j = i // 2
        if i % 2 == 0:
            h = _mixer_a(h, g_mix[i], w_in_a[j], g_sgu[j], w_s[j], b_s[j], km, vm, g_qm[i],
                         w_out_a[j], tm=512)
        else:
            q_hi, q_lo, k, vt, kmean, y_mem = _inproj_b(
                h, g_mix[i], w_in_b[j], g_q_b[j], g_k_b[j], km, vm, g_qm[i], tm=512)
            y_attn = _moba(q_hi, q_lo, k, vt, kmean, tq=MOBA_BLOCK, group=8)
            h = _outproj(h, y_attn, y_mem, w_out_b[j], tm=512)
        h = _ffn(h, g_ffn[i], w_up[i], conv_w[i], conv_b[i], w_down[i], tm=512, ch=256)
    return h
```

```python
import functools
import math

import jax
import jax.numpy as jnp
from jax import lax
from jax.experimental import pallas as pl
from jax.experimental.pallas import tpu as pltpu

F32 = jnp.float32
BF16 = jnp.bfloat16

HEAD_DIM = 64
LANES = 128
BF16_ROWS = 16
N_MEM_HEADS = 4
MEM_WIDTH = N_MEM_HEADS * HEAD_DIM
CHUNK = 128
GROUP_DIM = 128
MOBA_BLOCK = 256
MOBA_TOPK = 3
V_ROWS = HEAD_DIM + BF16_ROWS
CONV_WIDTH = 3
HALO = 8
EPS = 1e-6
NEG = -1e30
VMEM_LIMIT = 56 * 1024 * 1024


def _nt(a, b):
    return lax.dot_general(a, b, (((1,), (1,)), ((), ())), preferred_element_type=F32)


def _dot(a, b):
    return jnp.dot(a, b, preferred_element_type=F32)


def _gelu(x):
    c = math.sqrt(2.0 / math.pi)
    return 0.5 * x * (1.0 + jnp.tanh(c * (x + 0.044715 * (x * x * x))))


def _rms(x, g):
    return x * lax.rsqrt(jnp.mean(x * x, axis=-1, keepdims=True) + EPS) * g


def _pair_rms(x):
    lane = lax.broadcasted_iota(jnp.int32, (1, LANES), 1)
    lo = lane < HEAD_DIM
    outs = []
    for c in range(x.shape[1] // LANES):
        xc = x[:, c * LANES:(c + 1) * LANES]
        x2 = xc * xc
        s_lo = jnp.sum(jnp.where(lo, x2, 0.0), axis=-1, keepdims=True)
        s_hi = jnp.sum(jnp.where(lo, 0.0, x2), axis=-1, keepdims=True)
        ms = jnp.where(lo, s_lo, s_hi) * (1.0 / HEAD_DIM)
        outs.append(xc * lax.rsqrt(ms + EPS))
    return jnp.concatenate(outs, axis=-1)


def _mem_heads(qm, gq, km_ref, vm_ref):
    q = (_pair_rms(qm) * (gq * HEAD_DIM ** -0.5)).astype(BF16)
    outs = []
    for hh in range(N_MEM_HEADS):
        ls = slice(hh * HEAD_DIM, (hh + 1) * HEAD_DIM)
        s = _nt(q[:, ls], km_ref[0, :, ls])
        e = jnp.exp(s - jnp.max(s, axis=-1, keepdims=True))
        l = jnp.sum(e, axis=-1, keepdims=True)
        outs.append(_dot(e.astype(BF16), vm_ref[0, :, ls]) / l)
    return jnp.concatenate(outs, axis=-1)


def _mem_kv_kernel(mem_ref, g_ref, w_ref, gk_ref, km_ref, vm_ref):
    xn = _rms(mem_ref[0], g_ref[...]).astype(BF16)
    kv = _dot(xn, w_ref[...])
    km_ref[0] = (_pair_rms(kv[:, :MEM_WIDTH]) * gk_ref[...]).astype(BF16)
    vm_ref[0] = kv[:, MEM_WIDTH:].astype(BF16)


def _mem_kv(mem, g_mem, w_mem_kv, g_km):
    B, M, D = mem.shape
    gk = jnp.tile(g_km, N_MEM_HEADS)[None]
    full = lambda shape: pl.BlockSpec(shape, lambda b: (0,) * len(shape))
    return pl.pallas_call(
        _mem_kv_kernel,
        grid=(B,),
        in_specs=[pl.BlockSpec((1, M, D), lambda b: (b, 0, 0)),
                  full((1, D)), full((D, 2 * MEM_WIDTH)), full((1, MEM_WIDTH))],
        out_specs=[pl.BlockSpec((1, M, MEM_WIDTH), lambda b: (b, 0, 0))] * 2,
        out_shape=[jax.ShapeDtypeStruct((B, M, MEM_WIDTH), BF16)] * 2,
        name="mem_kv",
    )(mem, g_mem[None], w_mem_kv.astype(BF16), gk)


def _mixer_a_kernel(h_ref, g_ref, win_ref, gsgu_ref, ws_ref, bias_ref, km_ref, vm_ref,
                    gqm_ref, wout_ref, o_ref, ycat_ref, *, mixer_width):
    x = h_ref[0]
    tm = x.shape[0]
    n_groups = mixer_width // GROUP_DIM
    n_chunks = tm // CHUNK
    xn = _rms(x, g_ref[...]).astype(BF16)
    p = _dot(xn, win_ref[...])
    z = _gelu(p[:, :2 * mixer_width])
    u = z[:, :mixer_width]
    vn = _rms(z[:, mixer_width:], gsgu_ref[...]).astype(BF16)
    row = lax.broadcasted_iota(jnp.int32, (CHUNK, CHUNK), 0)
    col = lax.broadcasted_iota(jnp.int32, (CHUNK, CHUNK), 1)
    causal = row >= col
    for g in range(n_groups):
        gs = slice(g * GROUP_DIM, (g + 1) * GROUP_DIM)
        wc = jnp.where(causal, ws_ref[g], 0.0).astype(BF16)
        vg = jnp.concatenate([vn[c * CHUNK:(c + 1) * CHUNK, gs] for c in range(n_chunks)], axis=1)
        mixed = _dot(wc, vg)
        for c in range(n_chunks):
            rs = slice(c * CHUNK, (c + 1) * CHUNK)
            m_c = mixed[:, c * GROUP_DIM:(c + 1) * GROUP_DIM] + bias_ref[:, gs]
            ycat_ref[rs, gs] = (u[rs, gs] * m_c).astype(BF16)
    y_mem = _mem_heads(p[:, 2 * mixer_width:], gqm_ref[...], km_ref, vm_ref)
    ycat_ref[:, mixer_width:] = y_mem.astype(BF16)
    o_ref[0] = x + _dot(ycat_ref[...], wout_ref[...])


def _mixer_a(h, g_mix, w_in, g_sgu, w_s, b_s, km, vm, g_qm, w_out, *, tm):
    B, S, D = h.shape
    mixer_width = g_sgu.shape[0]
    n_in = w_in.shape[1]
    M = km.shape[1]
    bias = jnp.repeat(b_s.T, GROUP_DIM, axis=1)
    gqm = jnp.tile(g_qm, N_MEM_HEADS)[None]
    full = lambda shape: pl.BlockSpec(shape, lambda b, i: (0,) * len(shape))
    return pl.pallas_call(
        functools.partial(_mixer_a_kernel, mixer_width=mixer_width),
        grid=(B, S // tm),
        in_specs=[pl.BlockSpec((1, tm, D), lambda b, i: (b, i, 0)),
                  full((1, D)), full((D, n_in)), full((1, mixer_width)),
                  full(w_s.shape), full((CHUNK, mixer_width)),
                  pl.BlockSpec((1, M, MEM_WIDTH), lambda b, i: (b, 0, 0)),
                  pl.BlockSpec((1, M, MEM_WIDTH), lambda b, i: (b, 0, 0)),
                  full((1, MEM_WIDTH)), full((D, D))],
        out_specs=pl.BlockSpec((1, tm, D), lambda b, i: (b, i, 0)),
        out_shape=jax.ShapeDtypeStruct((B, S, D), F32),
        scratch_shapes=[pltpu.VMEM((tm, D), BF16)],
        compiler_params=pltpu.CompilerParams(
            dimension_semantics=("parallel", "parallel"), vmem_limit_bytes=VMEM_LIMIT),
        name="mixer_a",
    )(h, g_mix[None], w_in.astype(BF16), g_sgu[None], w_s, bias, km, vm, gqm, w_out.astype(BF16))


def _ffn_kernel(h_ref, halo_ref, g_ref, wup_ref, cw_ref, cb_ref, wdn_ref, o_ref, a_ref,
                *, d_ff, ch):
    i = pl.program_id(1)
    x = h_ref[0]
    tm = x.shape[0]
    g = g_ref[...]
    xn = _rms(x, g).astype(BF16)
    xh = _rms(halo_ref[0], g)
    xh = jnp.where(i > 0, xh, 0.0).astype(BF16)
    xe = jnp.concatenate([xh, xn], axis=0)

    def conv_part(cols):
        hh = _dot(xe, wup_ref[:, cols])
        w = cw_ref[:, cols]
        return (w[2:3] * hh[HALO:] + w[1:2] * hh[HALO - 1:HALO - 1 + tm]
                + w[0:1] * hh[HALO - 2:HALO - 2 + tm] + cb_ref[:, cols])

    for c in range(d_ff // ch):
        gate = conv_part(slice(c * ch, (c + 1) * ch))
        val = conv_part(slice(d_ff + c * ch, d_ff + (c + 1) * ch))
        a_ref[:, c * ch:(c + 1) * ch] = (_gelu(gate) * val).astype(BF16)
    o_ref[0] = x + _dot(a_ref[...], wdn_ref[...])


def _ffn(h, g_ffn, w_up, conv_w, conv_b, w_down, *, tm, ch):
    B, S, D = h.shape
    d_ff = w_down.shape[0]
    full = lambda shape: pl.BlockSpec(shape, lambda b, i: (0,) * len(shape))
    hb = tm // HALO
    return pl.pallas_call(
        functools.partial(_ffn_kernel, d_ff=d_ff, ch=ch),
        grid=(B, S // tm),
        in_specs=[pl.BlockSpec((1, tm, D), lambda b, i: (b, i, 0)),
                  pl.BlockSpec((1, HALO, D), lambda b, i: (b, jnp.maximum(i * hb - 1, 0), 0)),
                  full((1, D)), full((D, 2 * d_ff)), full((CONV_WIDTH, 2 * d_ff)),
                  full((1, 2 * d_ff)), full((d_ff, D))],
        out_specs=pl.BlockSpec((1, tm, D), lambda b, i: (b, i, 0)),
        out_shape=jax.ShapeDtypeStruct((B, S, D), F32),
        scratch_shapes=[pltpu.VMEM((tm, d_ff), BF16)],
        compiler_params=pltpu.CompilerParams(
            dimension_semantics=("parallel", "parallel"), vmem_limit_bytes=VMEM_LIMIT),
        name="ffn",
    )(h, h, g_ffn[None], w_up.astype(BF16), conv_w, conv_b[None], w_down.astype(BF16))


def _inproj_b_kernel(h_ref, g_ref, wqk_ref, wvt_ref, wmem_ref, gq_ref, gk_ref, km_ref, vm_ref,
                     gqm_ref, qhi_ref, qlo_ref, k_ref, vt_ref, kmean_ref, ymem_ref):
    x = h_ref[0]
    tm = x.shape[0]
    xn = _rms(x, g_ref[...]).astype(BF16)
    width = gq_ref.shape[1]
    n_heads = width // HEAD_DIM
    qk = _dot(xn, wqk_ref[...])
    q = _pair_rms(qk[:, :width]) * (gq_ref[...] * HEAD_DIM ** -0.5)
    k = _pair_rms(qk[:, width:]) * gk_ref[...]
    q_hi = q.astype(BF16)
    qhi_ref[0] = q_hi
    qlo_ref[0] = (q - q_hi.astype(F32)).astype(BF16)
    k_ref[0] = k.astype(BF16)
    nb = tm // MOBA_BLOCK
    kmean_ref[0] = jnp.concatenate(
        [jnp.mean(k[r * MOBA_BLOCK:(r + 1) * MOBA_BLOCK], axis=0, keepdims=True) for r in range(nb)],
        axis=0)
    vt = _nt(wvt_ref[...], xn)
    tail = jnp.where(lax.broadcasted_iota(jnp.int32, (BF16_ROWS, tm), 0) == 0, 1.0, 0.0)
    pieces = []
    for h in range(n_heads):
        pieces += [vt[h * HEAD_DIM:(h + 1) * HEAD_DIM], tail]
    vt = jnp.concatenate(pieces, axis=0).astype(BF16)
    for r in range(nb):
        vt_ref[0, r] = vt[:, r * MOBA_BLOCK:(r + 1) * MOBA_BLOCK]
    ymem_ref[0] = _mem_heads(_dot(xn, wmem_ref[...]), gqm_ref[...], km_ref, vm_ref).astype(BF16)


def _inproj_b(h, g_mix, w_in, g_q, g_k, km, vm, g_qm, *, tm):
    B, S, D = h.shape
    width = (w_in.shape[1] - MEM_WIDTH) // 3
    n_heads = width // HEAD_DIM
    M = km.shape[1]
    nt = S // tm
    nb = tm // MOBA_BLOCK
    w_qk = w_in[:, :2 * width].astype(BF16)
    w_vt = w_in[:, 2 * width:3 * width].T.astype(BF16)
    w_mem = w_in[:, 3 * width:].astype(BF16)
    gq = jnp.tile(g_q, n_heads)[None]
    gk = jnp.tile(g_k, n_heads)[None]
    gqm = jnp.tile(g_qm, N_MEM_HEADS)[None]
    full = lambda shape: pl.BlockSpec(shape, lambda b, i: (0,) * len(shape))
    tile = lambda w: pl.BlockSpec((1, tm, w), lambda b, i: (b, i, 0))
    outs = pl.pallas_call(
        _inproj_b_kernel,
        grid=(B, nt),
        in_specs=[tile(D), full((1, D)), full(w_qk.shape), full(w_vt.shape), full(w_mem.shape),
                  full((1, width)), full((1, width)),
                  pl.BlockSpec((1, M, MEM_WIDTH), lambda b, i: (b, 0, 0)),
                  pl.BlockSpec((1, M, MEM_WIDTH), lambda b, i: (b, 0, 0)),
                  full((1, MEM_WIDTH))],
        out_specs=[tile(width), tile(width), tile(width),
                   pl.BlockSpec((1, nb, n_heads * V_ROWS, MOBA_BLOCK), lambda b, i: (b, i, 0, 0)),
                   pl.BlockSpec((1, nb, width), lambda b, i: (b * nt + i, 0, 0)),
                   tile(MEM_WIDTH)],
        out_shape=[jax.ShapeDtypeStruct((B, S, width), BF16),
                   jax.ShapeDtypeStruct((B, S, width), BF16),
                   jax.ShapeDtypeStruct((B, S, width), BF16),
                   jax.ShapeDtypeStruct((B, S // MOBA_BLOCK, n_heads * V_ROWS, MOBA_BLOCK), BF16),
                   jax.ShapeDtypeStruct((B * nt, nb, width), F32),
                   jax.ShapeDtypeStruct((B, S, MEM_WIDTH), BF16)],
        compiler_params=pltpu.CompilerParams(
            dimension_semantics=("parallel", "parallel"), vmem_limit_bytes=VMEM_LIMIT),
        name="inproj_b",
    )(h, g_mix[None], w_qk, w_vt, w_mem, gq, gk, km, vm, gqm)
    q_hi, q_lo, k, vt, kmean, y_mem = outs
    return q_hi, q_lo, k, vt, kmean.reshape(B, S // MOBA_BLOCK, width), y_mem


def _moba_kernel(qhi_ref, qlo_ref, k_ref, vt_ref, kmean_ref, o_ref, bias_ref, s_ref, mx_ref, *,
                 group):
    own = pl.program_id(2)
    tq = qhi_ref.shape[1]
    nb = kmean_ref.shape[1]
    span = group * MOBA_BLOCK
    last_group = nb // group - 1
    rows = [slice(i * MOBA_BLOCK, (i + 1) * MOBA_BLOCK) for i in range(group)]
    cols = [slice(hd * tq, (hd + 1) * tq) for hd in range(2)]
    lane = lax.broadcasted_iota(jnp.int32, (1, LANES), 1)
    zero = jnp.zeros((), BF16)

    def both(x):
        return jnp.concatenate([jnp.where(lane < HEAD_DIM, x, zero), jnp.where(lane >= HEAD_DIM, x, zero)],
                               axis=0)

    q_both, q_lo = both(qhi_ref[0]), both(qlo_ref[0])
    km = kmean_ref[0]
    km_hi = km.astype(BF16)
    km_lo = (km - km_hi.astype(F32)).astype(BF16)

    def scores_into(g, i):
        start = pl.multiple_of((jnp.minimum(g, last_group) * group + i) * MOBA_BLOCK, MOBA_BLOCK)
        s = _nt(k_ref[0, pl.ds(start, MOBA_BLOCK), :], q_both)
        s_ref[rows[i], :] = s
        mx_ref[i:i + 1, :] = jnp.max(s, axis=0, keepdims=True)

    gate = _nt(km_hi, q_both) + (_nt(km_hi, q_lo) + _nt(km_lo, q_both))
    k_own = k_ref[0, pl.ds(pl.multiple_of(own * MOBA_BLOCK, MOBA_BLOCK), MOBA_BLOCK), :]
    s_own = _nt(k_own, q_both)
    for i in range(group - 1):
        scores_into(0, i)

    blk = lax.broadcasted_iota(jnp.int32, (nb, 2 * tq), 0)
    gate = jnp.where(blk < own, gate, NEG)
    sel = jnp.zeros(gate.shape, jnp.bool_)
    for j in range(MOBA_TOPK):
        mx = jnp.max(gate, axis=0, keepdims=True)
        first = jnp.min(jnp.where(gate == mx, blk, nb), axis=0, keepdims=True)
        pick = blk == first
        sel = jnp.logical_or(sel, jnp.logical_and(pick, j < own))
        gate = jnp.where(pick, -jnp.inf, gate)
    bias_ref[...] = jnp.where(sel, 0.0, NEG)

    key = lax.broadcasted_iota(jnp.int32, (MOBA_BLOCK, 2 * tq), 0)
    qry = lax.broadcasted_iota(jnp.int32, (MOBA_BLOCK, 2 * tq), 1)
    s_own = jnp.where(key <= jnp.where(qry < tq, qry, qry - tq), s_own, NEG)
    m0 = jnp.max(s_own, axis=0, keepdims=True)
    p_own = jnp.exp(s_own - m0).astype(BF16)
    init = tuple((m0[:, cols[hd]], _dot(vt_ref[0, own, hd * V_ROWS:(hd + 1) * V_ROWS, :], p_own[:, cols[hd]]))
                 for hd in range(2))

    def body(g, state):
        last = group - 1
        scores_into(g, last)

        def probs_times_values(i, hd, shift):
            p = jnp.exp((s_ref[rows[i], cols[hd]] + shift).astype(BF16))
            return _dot(vt_ref[0, g * group + i, hd * V_ROWS:(hd + 1) * V_ROWS, :], p)

        bias, m_new, acc = [], [], []
        for hd in range(2):
            m, a = state[hd]
            b = [bias_ref[pl.ds(g * group + i, 1), cols[hd]] for i in range(group)]
            mx = m
            for i in range(last):
                mx = jnp.maximum(mx, mx_ref[i:i + 1, cols[hd]] + b[i])
            bias.append(b)
            m_new.append(mx)
            acc.append(jnp.exp(m - mx) * a)
        for i in range(last):
            for hd in range(2):
                acc[hd] = acc[hd] + probs_times_values(i, hd, bias[hd][i] - m_new[hd])
            scores_into(g + 1, i)
        out = []
        for hd in range(2):
            mx = jnp.maximum(m_new[hd], mx_ref[last:group, cols[hd]] + bias[hd][last])
            a = jnp.exp(m_new[hd] - mx) * acc[hd]
            out.append((mx, a + probs_times_values(last, hd, bias[hd][last] - mx)))
        return tuple(out)

    final = lax.fori_loop(0, (own + group - 1) // group, body, init)
    out_t = jnp.concatenate(
        [acc[:HEAD_DIM] / acc[HEAD_DIM:HEAD_DIM + 1] for _, acc in final], axis=0)
    o_ref[0] = out_t.T.astype(o_ref.dtype)


def _moba(q_hi, q_lo, k, vt, kmean, *, tq, group):
    B, S, width = q_hi.shape
    nb = S // MOBA_BLOCK
    n_pairs = width // LANES
    assert nb % group == 0 and tq == MOBA_BLOCK
    pair_tile = pl.BlockSpec((1, tq, LANES), lambda b, hp, t: (b, t, hp))
    return pl.pallas_call(
        functools.partial(_moba_kernel, group=group),
        grid=(B, n_pairs, S // tq),
        in_specs=[pair_tile, pair_tile,
                  pl.BlockSpec((1, S, LANES), lambda b, hp, t: (b, 0, hp)),
                  pl.BlockSpec((1, nb, 2 * V_ROWS, MOBA_BLOCK), lambda b, hp, t: (b, 0, hp, 0)),
                  pl.BlockSpec((1, nb, LANES), lambda b, hp, t: (b, 0, hp))],
        out_specs=pair_tile,
        out_shape=jax.ShapeDtypeStruct((B, S, width), BF16),
        scratch_shapes=[pltpu.VMEM((nb, 2 * tq), F32),
                        pltpu.VMEM((group * MOBA_BLOCK, 2 * tq), F32),
                        pltpu.VMEM((group, 2 * tq), F32)],
        compiler_params=pltpu.CompilerParams(
            dimension_semantics=("parallel", "parallel", "arbitrary"),
            vmem_limit_bytes=VMEM_LIMIT),
        name="moba",
    )(q_hi, q_lo, k, vt, kmean)


def _outproj_kernel(h_ref, ya_ref, ym_ref, w_ref, o_ref):
    wa = ya_ref.shape[2]
    y = _dot(ya_ref[0], w_ref[:wa, :]) + _dot(ym_ref[0], w_ref[wa:, :])
    o_ref[0] = h_ref[0] + y


def _outproj(h, y_attn, y_mem, w_out, *, tm):
    B, S, D = h.shape
    tile = lambda w: pl.BlockSpec((1, tm, w), lambda b, i: (b, i, 0))
    return pl.pallas_call(
        _outproj_kernel,
        grid=(B, S // tm),
        in_specs=[tile(D), tile(y_attn.shape[2]), tile(y_mem.shape[2]),
                  pl.BlockSpec(w_out.shape, lambda b, i: (0, 0))],
        out_specs=tile(D),
        out_shape=jax.ShapeDtypeStruct((B, S, D), F32),
        compiler_params=pltpu.CompilerParams(
            dimension_semantics=("parallel", "parallel"), vmem_limit_bytes=VMEM_LIMIT),
        name="outproj_b",
    )(h, y_attn, y_mem, w_out.astype(BF16))


def kernel(x, mem, g_mix, g_ffn, w_in_a, w_out_a, g_sgu, w_s, b_s, w_in_b, w_out_b, g_q_b,
           g_k_b, g_mem, w_mem_kv, g_km, g_qm, w_up, conv_w, conv_b, w_down):
    B, S, D = x.shape
    depth = g_mix.shape[0]
    assert S % MOBA_BLOCK == 0
    km, vm = _mem_kv(mem, g_mem, w_mem_kv, g_km)
    h = x
    for i in range(depth):
        j = i // 2
        if i % 2 == 0:
            h = _mixer_a(h, g_mix[i], w_in_a[j], g_sgu[j], w_s[j], b_s[j], km, vm, g_qm[i],
                         w_out_a[j], tm=512)
        else:
            q_hi, q_lo, k, vt, kmean, y_mem = _inproj_b(
                h, g_mix[i], w_in_b[j], g_q_b[j], g_k_b[j], km, vm, g_qm[i], tm=512)
            y_attn = _moba(q_hi, q_lo, k, vt, kmean, tq=MOBA_BLOCK, group=8)
            h = _outproj(h, y_attn, y_mem, w_out_b[j], tm=512)
        h = _ffn(h, g_ffn[i], w_up[i], conv_w[i], conv_b[i], w_down[i], tm=512, ch=256)
    return h
```

```python
import functools
import math

import jax
import jax.numpy as jnp
from jax import lax
from jax.experimental import pallas as pl
from jax.experimental.pallas import tpu as pltpu

F32 = jnp.float32
BF16 = jnp.bfloat16

HEAD_DIM = 64
LANES = 128
BF16_ROWS = 16
N_MEM_HEADS = 4
MEM_WIDTH = N_MEM_HEADS * HEAD_DIM
CHUNK = 128
GROUP_DIM = 128
MOBA_BLOCK = 256
MOBA_TOPK = 3
V_ROWS = HEAD_DIM + BF16_ROWS
CONV_WIDTH = 3
HALO = 8
EPS = 1e-6
NEG = -1e30
VMEM_LIMIT = 56 * 1024 * 1024


def _nt(a, b):
    return lax.dot_general(a, b, (((1,), (1,)), ((), ())), preferred_element_type=F32)


def _dot(a, b):
    return jnp.dot(a, b, preferred_element_type=F32)


def _gelu(x):
    c = math.sqrt(2.0 / math.pi)
    return 0.5 * x * (1.0 + jnp.tanh(c * (x + 0.044715 * (x * x * x))))


def _rms(x, g):
    return x * lax.rsqrt(jnp.mean(x * x, axis=-1, keepdims=True) + EPS) * g


def _pair_rms(x):
    lane = lax.broadcasted_iota(jnp.int32, (1, LANES), 1)
    lo = lane < HEAD_DIM
    outs = []
    for c in range(x.shape[1] // LANES):
        xc = x[:, c * LANES:(c + 1) * LANES]
        x2 = xc * xc
        s_lo = jnp.sum(jnp.where(lo, x2, 0.0), axis=-1, keepdims=True)
        s_hi = jnp.sum(jnp.where(lo, 0.0, x2), axis=-1, keepdims=True)
        ms = jnp.where(lo, s_lo, s_hi) * (1.0 / HEAD_DIM)
        outs.append(xc * lax.rsqrt(ms + EPS))
    return jnp.concatenate(outs, axis=-1)


def _mem_heads(qm, gq, km_ref, vm_ref):
    q = (_pair_rms(qm) * (gq * HEAD_DIM ** -0.5)).astype(BF16)
    outs = []
    for hh in range(N_MEM_HEADS):
        ls = slice(hh * HEAD_DIM, (hh + 1) * HEAD_DIM)
        s = _nt(q[:, ls], km_ref[0, :, ls])
        e = jnp.exp(s - jnp.max(s, axis=-1, keepdims=True))
        l = jnp.sum(e, axis=-1, keepdims=True)
        outs.append(_dot(e.astype(BF16), vm_ref[0, :, ls]) / l)
    return jnp.concatenate(outs, axis=-1)


def _mem_kv_kernel(mem_ref, g_ref, w_ref, gk_ref, km_ref, vm_ref):
    xn = _rms(mem_ref[0], g_ref[...]).astype(BF16)
    kv = _dot(xn, w_ref[...])
    km_ref[0] = (_pair_rms(kv[:, :MEM_WIDTH]) * gk_ref[...]).astype(BF16)
    vm_ref[0] = kv[:, MEM_WIDTH:].astype(BF16)


def _mem_kv(mem, g_mem, w_mem_kv, g_km):
    B, M, D = mem.shape
    gk = jnp.tile(g_km, N_MEM_HEADS)[None]
    full = lambda shape: pl.BlockSpec(shape, lambda b: (0,) * len(shape))
    return pl.pallas_call(
        _mem_kv_kernel,
        grid=(B,),
        in_specs=[pl.BlockSpec((1, M, D), lambda b: (b, 0, 0)),
                  full((1, D)), full((D, 2 * MEM_WIDTH)), full((1, MEM_WIDTH))],
        out_specs=[pl.BlockSpec((1, M, MEM_WIDTH), lambda b: (b, 0, 0))] * 2,
        out_shape=[jax.ShapeDtypeStruct((B, M, MEM_WIDTH), BF16)] * 2,
        name="mem_kv",
    )(mem, g_mem[None], w_mem_kv.astype(BF16), gk)


def _mixer_a_kernel(h_ref, g_ref, win_ref, gsgu_ref, ws_ref, bias_ref, km_ref, vm_ref,
                    gqm_ref, wout_ref, o_ref, ycat_ref, *, mixer_width):
    x = h_ref[0]
    tm = x.shape[0]
    n_groups = mixer_width // GROUP_DIM
    n_chunks = tm // CHUNK
    xn = _rms(x, g_ref[...]).astype(BF16)
    p = _dot(xn, win_ref[...])
    z = _gelu(p[:, :2 * mixer_width])
    u = z[:, :mixer_width]
    vn = _rms(z[:, mixer_width:], gsgu_ref[...]).astype(BF16)
    row = lax.broadcasted_iota(jnp.int32, (CHUNK, CHUNK), 0)
    col = lax.broadcasted_iota(jnp.int32, (CHUNK, CHUNK), 1)
    causal = row >= col
    for g in range(n_groups):
        gs = slice(g * GROUP_DIM, (g + 1) * GROUP_DIM)
        wc = jnp.where(causal, ws_ref[g], 0.0).astype(BF16)
        vg = jnp.concatenate([vn[c * CHUNK:(c + 1) * CHUNK, gs] for c in range(n_chunks)], axis=1)
        mixed = _dot(wc, vg)
        for c in range(n_chunks):
            rs = slice(c * CHUNK, (c + 1) * CHUNK)
            m_c = mixed[:, c * GROUP_DIM:(c + 1) * GROUP_DIM] + bias_ref[:, gs]
            ycat_ref[rs, gs] = (u[rs, gs] * m_c).astype(BF16)
    y_mem = _mem_heads(p[:, 2 * mixer_width:], gqm_ref[...], km_ref, vm_ref)
    ycat_ref[:, mixer_width:] = y_mem.astype(BF16)
    o_ref[0] = x + _dot(ycat_ref[...], wout_ref[...])


def _mixer_a(h, g_mix, w_in, g_sgu, w_s, b_s, km, vm, g_qm, w_out, *, tm):
    B, S, D = h.shape
    mixer_width = g_sgu.shape[0]
    n_in = w_in.shape[1]
    M = km.shape[1]
    bias = jnp.repeat(b_s.T, GROUP_DIM, axis=1)
    gqm = jnp.tile(g_qm, N_MEM_HEADS)[None]
    full = lambda shape: pl.BlockSpec(shape, lambda b, i: (0,) * len(shape))
    return pl.pallas_call(
        functools.partial(_mixer_a_kernel, mixer_width=mixer_width),
        grid=(B, S // tm),
        in_specs=[pl.BlockSpec((1, tm, D), lambda b, i: (b, i, 0)),
                  full((1, D)), full((D, n_in)), full((1, mixer_width)),
                  full(w_s.shape), full((CHUNK, mixer_width)),
                  pl.BlockSpec((1, M, MEM_WIDTH), lambda b, i: (b, 0, 0)),
                  pl.BlockSpec((1, M, MEM_WIDTH), lambda b, i: (b, 0, 0)),
                  full((1, MEM_WIDTH)), full((D, D))],
        out_specs=pl.BlockSpec((1, tm, D), lambda b, i: (b, i, 0)),
        out_shape=jax.ShapeDtypeStruct((B, S, D), F32),
        scratch_shapes=[pltpu.VMEM((tm, D), BF16)],
        compiler_params=pltpu.CompilerParams(
            dimension_semantics=("parallel", "parallel"), vmem_limit_bytes=VMEM_LIMIT),
        name="mixer_a",
    )(h, g_mix[None], w_in.astype(BF16), g_sgu[None], w_s, bias, km, vm, gqm, w_out.astype(BF16))


def _ffn_kernel(*refs, d_ff, ch, n_pre):
    h_ref, halo_ref = refs[:2]
    pre = refs[2:2 + 2 * n_pre]
    wout_ref = refs[2 + 2 * n_pre] if n_pre else None
    g_ref, wup_ref, cw_ref, cb_ref, wdn_ref, o_ref, a_ref = refs[len(refs) - 7:]
    i = pl.program_id(1)
    x = h_ref[0]
    xh = halo_ref[0]
    tm = x.shape[0]
    row0 = 0
    for j in range(n_pre):
        y_ref, yh_ref = pre[2 * j], pre[2 * j + 1]
        w = wout_ref[row0:row0 + y_ref.shape[2], :]
        row0 += y_ref.shape[2]
        x = x + _dot(y_ref[0], w)
        xh = xh + _dot(yh_ref[0], w)[BF16_ROWS - HALO:]
    g = g_ref[...]
    xn = _rms(x, g).astype(BF16)
    xh = _rms(xh, g)
    xh = jnp.where(i > 0, xh, 0.0).astype(BF16)
    xe = jnp.concatenate([xh, xn], axis=0)

    def conv_part(cols):
        hh = _dot(xe, wup_ref[:, cols])
        w = cw_ref[:, cols]
        return (w[2:3] * hh[HALO:] + w[1:2] * hh[HALO - 1:HALO - 1 + tm]
                + w[0:1] * hh[HALO - 2:HALO - 2 + tm] + cb_ref[:, cols])

    for c in range(d_ff // ch):
        gate = conv_part(slice(c * ch, (c + 1) * ch))
        val = conv_part(slice(d_ff + c * ch, d_ff + (c + 1) * ch))
        a_ref[:, c * ch:(c + 1) * ch] = (_gelu(gate) * val).astype(BF16)
    o_ref[0] = x + _dot(a_ref[...], wdn_ref[...])


def _ffn(h, g_ffn, w_up, conv_w, conv_b, w_down, *, tm, ch, pending=(), w_out=None):
    B, S, D = h.shape
    d_ff = w_down.shape[0]
    full = lambda shape: pl.BlockSpec(shape, lambda b, i: (0,) * len(shape))

    def tile_and_halo(width, halo_rows):
        per_tile = tm // halo_rows
        return [pl.BlockSpec((1, tm, width), lambda b, i: (b, i, 0)),
                pl.BlockSpec((1, halo_rows, width), lambda b, i: (b, jnp.maximum(i * per_tile - 1, 0), 0))]

    in_specs = tile_and_halo(D, HALO)
    args = [h, h]
    for y in pending:
        in_specs += tile_and_halo(y.shape[2], BF16_ROWS)
        args += [y, y]
    if pending:
        in_specs.append(full(w_out.shape))
        args.append(w_out.astype(BF16))
    in_specs += [full((1, D)), full((D, 2 * d_ff)), full((CONV_WIDTH, 2 * d_ff)),
                 full((1, 2 * d_ff)), full((d_ff, D))]
    args += [g_ffn[None], w_up.astype(BF16), conv_w, conv_b[None], w_down.astype(BF16)]
    return pl.pallas_call(
        functools.partial(_ffn_kernel, d_ff=d_ff, ch=ch, n_pre=len(pending)),
        grid=(B, S // tm),
        in_specs=in_specs,
        out_specs=pl.BlockSpec((1, tm, D), lambda b, i: (b, i, 0)),
        out_shape=jax.ShapeDtypeStruct((B, S, D), F32),
        scratch_shapes=[pltpu.VMEM((tm, d_ff), BF16)],
        compiler_params=pltpu.CompilerParams(
            dimension_semantics=("parallel", "parallel"), vmem_limit_bytes=VMEM_LIMIT),
        name="ffn",
    )(*args)


def _inproj_b_kernel(h_ref, g_ref, wqk_ref, wvt_ref, wmem_ref, gq_ref, gk_ref, km_ref, vm_ref,
                     gqm_ref, qhi_ref, qlo_ref, k_ref, vt_ref, kmean_ref, ymem_ref):
    x = h_ref[0]
    tm = x.shape[0]
    xn = _rms(x, g_ref[...]).astype(BF16)
    width = gq_ref.shape[1]
    n_heads = width // HEAD_DIM
    qk = _dot(xn, wqk_ref[...])
    q = _pair_rms(qk[:, :width]) * (gq_ref[...] * HEAD_DIM ** -0.5)
    k = _pair_rms(qk[:, width:]) * gk_ref[...]
    q_hi = q.astype(BF16)
    qhi_ref[0] = q_hi
    qlo_ref[0] = (q - q_hi.astype(F32)).astype(BF16)
    k_ref[0] = k.astype(BF16)
    nb = tm // MOBA_BLOCK
    kmean_ref[0] = jnp.concatenate(
        [jnp.mean(k[r * MOBA_BLOCK:(r + 1) * MOBA_BLOCK], axis=0, keepdims=True) for r in range(nb)],
        axis=0)
    vt = _nt(wvt_ref[...], xn)
    tail = jnp.where(lax.broadcasted_iota(jnp.int32, (BF16_ROWS, tm), 0) == 0, 1.0, 0.0)
    pieces = []
    for h in range(n_heads):
        pieces += [vt[h * HEAD_DIM:(h + 1) * HEAD_DIM], tail]
    vt = jnp.concatenate(pieces, axis=0).astype(BF16)
    for r in range(nb):
        vt_ref[0, r] = vt[:, r * MOBA_BLOCK:(r + 1) * MOBA_BLOCK]
    ymem_ref[0] = _mem_heads(_dot(xn, wmem_ref[...]), gqm_ref[...], km_ref, vm_ref).astype(BF16)


def _inproj_b(h, g_mix, w_in, g_q, g_k, km, vm, g_qm, *, tm):
    B, S, D = h.shape
    width = (w_in.shape[1] - MEM_WIDTH) // 3
    n_heads = width // HEAD_DIM
    M = km.shape[1]
    nt = S // tm
    nb = tm // MOBA_BLOCK
    w_qk = w_in[:, :2 * width].astype(BF16)
    w_vt = w_in[:, 2 * width:3 * width].T.astype(BF16)
    w_mem = w_in[:, 3 * width:].astype(BF16)
    gq = jnp.tile(g_q, n_heads)[None]
    gk = jnp.tile(g_k, n_heads)[None]
    gqm = jnp.tile(g_qm, N_MEM_HEADS)[None]
    full = lambda shape: pl.BlockSpec(shape, lambda b, i: (0,) * len(shape))
    tile = lambda w: pl.BlockSpec((1, tm, w), lambda b, i: (b, i, 0))
    outs = pl.pallas_call(
        _inproj_b_kernel,
        grid=(B, nt),
        in_specs=[tile(D), full((1, D)), full(w_qk.shape), full(w_vt.shape), full(w_mem.shape),
                  full((1, width)), full((1, width)),
                  pl.BlockSpec((1, M, MEM_WIDTH), lambda b, i: (b, 0, 0)),
                  pl.BlockSpec((1, M, MEM_WIDTH), lambda b, i: (b, 0, 0)),
                  full((1, MEM_WIDTH))],
        out_specs=[tile(width), tile(width), tile(width),
                   pl.BlockSpec((1, nb, n_heads * V_ROWS, MOBA_BLOCK), lambda b, i: (b, i, 0, 0)),
                   pl.BlockSpec((1, nb, width), lambda b, i: (b * nt + i, 0, 0)),
                   tile(MEM_WIDTH)],
        out_shape=[jax.ShapeDtypeStruct((B, S, width), BF16),
                   jax.ShapeDtypeStruct((B, S, width), BF16),
                   jax.ShapeDtypeStruct((B, S, width), BF16),
                   jax.ShapeDtypeStruct((B, S // MOBA_BLOCK, n_heads * V_ROWS, MOBA_BLOCK), BF16),
                   jax.ShapeDtypeStruct((B * nt, nb, width), F32),
                   jax.ShapeDtypeStruct((B, S, MEM_WIDTH), BF16)],
        compiler_params=pltpu.CompilerParams(
            dimension_semantics=("parallel", "parallel"), vmem_limit_bytes=VMEM_LIMIT),
        name="inproj_b",
    )(h, g_mix[None], w_qk, w_vt, w_mem, gq, gk, km, vm, gqm)
    q_hi, q_lo, k, vt, kmean, y_mem = outs
    return q_hi, q_lo, k, vt, kmean.reshape(B, S // MOBA_BLOCK, width), y_mem


def _moba_kernel(qhi_ref, qlo_ref, k_ref, vt_ref, kmean_ref, o_ref, bias_ref, s_ref, mx_ref, *,
                 group):
    own = pl.program_id(2)
    tq = qhi_ref.shape[1]
    nb = kmean_ref.shape[1]
    span = group * MOBA_BLOCK
    last_group = nb // group - 1
    rows = [slice(i * MOBA_BLOCK, (i + 1) * MOBA_BLOCK) for i in range(group)]
    cols = [slice(hd * tq, (hd + 1) * tq) for hd in range(2)]
    lane = lax.broadcasted_iota(jnp.int32, (1, LANES), 1)
    zero = jnp.zeros((), BF16)

    def both(x):
        return jnp.concatenate([jnp.where(lane < HEAD_DIM, x, zero), jnp.where(lane >= HEAD_DIM, x, zero)],
                               axis=0)

    q_both, q_lo = both(qhi_ref[0]), both(qlo_ref[0])
    km = kmean_ref[0]
    km_hi = km.astype(BF16)
    km_lo = (km - km_hi.astype(F32)).astype(BF16)

    def scores_into(g, i):
        start = pl.multiple_of((jnp.minimum(g, last_group) * group + i) * MOBA_BLOCK, MOBA_BLOCK)
        s = _nt(k_ref[0, pl.ds(start, MOBA_BLOCK), :], q_both)
        s_ref[rows[i], :] = s
        mx_ref[i:i + 1, :] = jnp.max(s, axis=0, keepdims=True)

    gate = _nt(km_hi, q_both) + (_nt(km_hi, q_lo) + _nt(km_lo, q_both))
    k_own = k_ref[0, pl.ds(pl.multiple_of(own * MOBA_BLOCK, MOBA_BLOCK), MOBA_BLOCK), :]
    s_own = _nt(k_own, q_both)
    for i in range(group - 1):
        scores_into(0, i)

    blk = lax.broadcasted_iota(jnp.int32, (nb, 2 * tq), 0)
    gate = jnp.where(blk < own, gate, NEG)
    sel = jnp.zeros(gate.shape, jnp.bool_)
    for j in range(MOBA_TOPK):
        mx = jnp.max(gate, axis=0, keepdims=True)
        first = jnp.min(jnp.where(gate == mx, blk, nb), axis=0, keepdims=True)
        pick = blk == first
        sel = jnp.logical_or(sel, jnp.logical_and(pick, j < own))
        gate = jnp.where(pick, -jnp.inf, gate)
    bias_ref[...] = jnp.where(sel, 0.0, NEG)

    key = lax.broadcasted_iota(jnp.int32, (MOBA_BLOCK, 2 * tq), 0)
    qry = lax.broadcasted_iota(jnp.int32, (MOBA_BLOCK, 2 * tq), 1)
    s_own = jnp.where(key <= jnp.where(qry < tq, qry, qry - tq), s_own, NEG)
    m0 = jnp.max(s_own, axis=0, keepdims=True)
    p_own = jnp.exp(s_own - m0).astype(BF16)
    init = tuple((m0[:, cols[hd]], _dot(vt_ref[0, own, hd * V_ROWS:(hd + 1) * V_ROWS, :], p_own[:, cols[hd]]))
                 for hd in range(2))

    def trip(g, state, refill, width=group):
        last = group - 1
        peeled = width == group
        if peeled:
            scores_into(g, last)
        main = range(last) if peeled else range(width)

        def probs_times_values(i, hd, shift):
            p = jnp.exp((s_ref[rows[i], cols[hd]] + shift).astype(BF16))
            return _dot(vt_ref[0, g * group + i, hd * V_ROWS:(hd + 1) * V_ROWS, :], p)

        bias, m_new, acc = [], [], []
        for hd in range(2):
            m, a = state[hd]
            b = [bias_ref[pl.ds(g * group + i, 1), cols[hd]] for i in range(width)]
            mx = m
            for i in main:
                mx = jnp.maximum(mx, mx_ref[i:i + 1, cols[hd]] + b[i])
            bias.append(b)
            m_new.append(mx)
            acc.append(jnp.exp(m - mx) * a)
        for i in main:
            for hd in range(2):
                acc[hd] = acc[hd] + probs_times_values(i, hd, bias[hd][i] - m_new[hd])
            if refill:
                scores_into(g + 1, i)
        if not peeled:
            return tuple(zip(m_new, acc))
        out = []
        for hd in range(2):
            mx = jnp.maximum(m_new[hd], mx_ref[last:group, cols[hd]] + bias[hd][last])
            a = jnp.exp(m_new[hd] - mx) * acc[hd]
            out.append((mx, a + probs_times_values(last, hd, bias[hd][last] - mx)))
        return tuple(out)

    n_trips = (own + group - 1) // group
    g_last = jnp.maximum(n_trips - 1, 0)
    short = own - g_last * group <= group // 2
    state = lax.fori_loop(0, n_trips - 1, functools.partial(trip, refill=True), init)
    state = lax.fori_loop(0, jnp.where(jnp.logical_and(n_trips > 0, jnp.logical_not(short)), 1, 0),
                          lambda _, st: trip(g_last, st, False), state)
    final = lax.fori_loop(0, jnp.where(jnp.logical_and(n_trips > 0, short), 1, 0),
                          lambda _, st: trip(g_last, st, False, group // 2), state)
    out_t = jnp.concatenate(
        [acc[:HEAD_DIM] / acc[HEAD_DIM:HEAD_DIM + 1] for _, acc in final], axis=0)
    o_ref[0] = out_t.T.astype(o_ref.dtype)


def _moba(q_hi, q_lo, k, vt, kmean, *, tq, group):
    B, S, width = q_hi.shape
    nb = S // MOBA_BLOCK
    n_pairs = width // LANES
    assert nb % group == 0 and tq == MOBA_BLOCK
    pair_tile = pl.BlockSpec((1, tq, LANES), lambda b, hp, t: (b, t, hp))
    return pl.pallas_call(
        functools.partial(_moba_kernel, group=group),
        grid=(B, n_pairs, S // tq),
        in_specs=[pair_tile, pair_tile,
                  pl.BlockSpec((1, S, LANES), lambda b, hp, t: (b, 0, hp)),
                  pl.BlockSpec((1, nb, 2 * V_ROWS, MOBA_BLOCK), lambda b, hp, t: (b, 0, hp, 0)),
                  pl.BlockSpec((1, nb, LANES), lambda b, hp, t: (b, 0, hp))],
        out_specs=pair_tile,
        out_shape=jax.ShapeDtypeStruct((B, S, width), BF16),
        scratch_shapes=[pltpu.VMEM((nb, 2 * tq), F32),
                        pltpu.VMEM((group * MOBA_BLOCK, 2 * tq), F32),
                        pltpu.VMEM((group, 2 * tq), F32)],
        compiler_params=pltpu.CompilerParams(
            dimension_semantics=("parallel", "parallel", "arbitrary"),
            vmem_limit_bytes=VMEM_LIMIT),
        name="moba",
    )(q_hi, q_lo, k, vt, kmean)


def kernel(x, mem, g_mix, g_ffn, w_in_a, w_out_a, g_sgu, w_s, b_s, w_in_b, w_out_b, g_q_b,
           g_k_b, g_mem, w_mem_kv, g_km, g_qm, w_up, conv_w, conv_b, w_down):
    B, S, D = x.shape
    depth = g_mix.shape[0]
    assert S % MOBA_BLOCK == 0
    km, vm = _mem_kv(mem, g_mem, w_mem_kv, g_km)
    h = x
    for i in range(depth):
        j = i // 2
        ffn = functools.partial(_ffn, g_ffn=g_ffn[i], w_up=w_up[i], conv_w=conv_w[i], conv_b=conv_b[i],
                                w_down=w_down[i], tm=512, ch=256)
        if i % 2 == 0:
            h = _mixer_a(h, g_mix[i], w_in_a[j], g_sgu[j], w_s[j], b_s[j], km, vm, g_qm[i],
                         w_out_a[j], tm=512)
            h = ffn(h)
        else:
            q_hi, q_lo, k, vt, kmean, y_mem = _inproj_b(
                h, g_mix[i], w_in_b[j], g_q_b[j], g_k_b[j], km, vm, g_qm[i], tm=512)
            y_attn = _moba(q_hi, q_lo, k, vt, kmean, tq=MOBA_BLOCK, group=8)
            h = ffn(h, pending=(y_attn, y_mem), w_out=w_out_b[j])
    return h
```

```python
import functools
import math

import jax
import jax.numpy as jnp
from jax import lax
from jax.experimental import pallas as pl
from jax.experimental.pallas import tpu as pltpu

F32 = jnp.float32
BF16 = jnp.bfloat16

HEAD_DIM = 64
LANES = 128
BF16_ROWS = 16
N_MEM_HEADS = 4
MEM_WIDTH = N_MEM_HEADS * HEAD_DIM
CHUNK = 128
GROUP_DIM = 128
MOBA_BLOCK = 256
MOBA_TOPK = 3
V_ROWS = HEAD_DIM + BF16_ROWS
CONV_WIDTH = 3
HALO = 8
EPS = 1e-6
NEG = -1e30
VMEM_LIMIT = 56 * 1024 * 1024


def _nt(a, b):
    return lax.dot_general(a, b, (((1,), (1,)), ((), ())), preferred_element_type=F32)


def _dot(a, b):
    return jnp.dot(a, b, preferred_element_type=F32)


def _gelu(x):
    c = math.sqrt(2.0 / math.pi)
    return 0.5 * x * (1.0 + jnp.tanh(c * (x + 0.044715 * (x * x * x))))


def _rms(x, g):
    return x * lax.rsqrt(jnp.mean(x * x, axis=-1, keepdims=True) + EPS) * g


def _pair_rms(x):
    lane = lax.broadcasted_iota(jnp.int32, (1, LANES), 1)
    lo = lane < HEAD_DIM
    outs = []
    for c in range(x.shape[1] // LANES):
        xc = x[:, c * LANES:(c + 1) * LANES]
        x2 = xc * xc
        s_lo = jnp.sum(jnp.where(lo, x2, 0.0), axis=-1, keepdims=True)
        s_hi = jnp.sum(jnp.where(lo, 0.0, x2), axis=-1, keepdims=True)
        ms = jnp.where(lo, s_lo, s_hi) * (1.0 / HEAD_DIM)
        outs.append(xc * lax.rsqrt(ms + EPS))
    return jnp.concatenate(outs, axis=-1)


def _mem_heads(qm, gq, km_ref, vm_ref):
    q = (_pair_rms(qm) * (gq * HEAD_DIM ** -0.5)).astype(BF16)
    outs = []
    for hh in range(N_MEM_HEADS):
        ls = slice(hh * HEAD_DIM, (hh + 1) * HEAD_DIM)
        s = _nt(q[:, ls], km_ref[0, :, ls])
        e = jnp.exp(s - jnp.max(s, axis=-1, keepdims=True))
        l = jnp.sum(e, axis=-1, keepdims=True)
        outs.append(_dot(e.astype(BF16), vm_ref[0, :, ls]) / l)
    return jnp.concatenate(outs, axis=-1)


def _mem_kv_kernel(mem_ref, g_ref, w_ref, gk_ref, km_ref, vm_ref):
    xn = _rms(mem_ref[0], g_ref[...]).astype(BF16)
    kv = _dot(xn, w_ref[...])
    km_ref[0] = (_pair_rms(kv[:, :MEM_WIDTH]) * gk_ref[...]).astype(BF16)
    vm_ref[0] = kv[:, MEM_WIDTH:].astype(BF16)


def _mem_kv(mem, g_mem, w_mem_kv, g_km):
    B, M, D = mem.shape
    gk = jnp.tile(g_km, N_MEM_HEADS)[None]
    full = lambda shape: pl.BlockSpec(shape, lambda b: (0,) * len(shape))
    return pl.pallas_call(
        _mem_kv_kernel,
        grid=(B,),
        in_specs=[pl.BlockSpec((1, M, D), lambda b: (b, 0, 0)),
                  full((1, D)), full((D, 2 * MEM_WIDTH)), full((1, MEM_WIDTH))],
        out_specs=[pl.BlockSpec((1, M, MEM_WIDTH), lambda b: (b, 0, 0))] * 2,
        out_shape=[jax.ShapeDtypeStruct((B, M, MEM_WIDTH), BF16)] * 2,
        name="mem_kv",
    )(mem, g_mem[None], w_mem_kv.astype(BF16), gk)


def _mixer_a_kernel(h_ref, g_ref, win_ref, gsgu_ref, ws_ref, bias_ref, km_ref, vm_ref,
                    gqm_ref, wout_ref, o_ref, ycat_ref, *, mixer_width):
    x = h_ref[0]
    tm = x.shape[0]
    n_groups = mixer_width // GROUP_DIM
    n_chunks = tm // CHUNK
    xn = _rms(x, g_ref[...]).astype(BF16)
    p = _dot(xn, win_ref[...])
    z = _gelu(p[:, :2 * mixer_width])
    u = z[:, :mixer_width]
    vn = _rms(z[:, mixer_width:], gsgu_ref[...]).astype(BF16)
    row = lax.broadcasted_iota(jnp.int32, (CHUNK, CHUNK), 0)
    col = lax.broadcasted_iota(jnp.int32, (CHUNK, CHUNK), 1)
    causal = row >= col
    for g in range(n_groups):
        gs = slice(g * GROUP_DIM, (g + 1) * GROUP_DIM)
        wc = jnp.where(causal, ws_ref[g], 0.0).astype(BF16)
        vg = jnp.concatenate([vn[c * CHUNK:(c + 1) * CHUNK, gs] for c in range(n_chunks)], axis=1)
        mixed = _dot(wc, vg)
        for c in range(n_chunks):
            rs = slice(c * CHUNK, (c + 1) * CHUNK)
            m_c = mixed[:, c * GROUP_DIM:(c + 1) * GROUP_DIM] + bias_ref[:, gs]
            ycat_ref[rs, gs] = (u[rs, gs] * m_c).astype(BF16)
    y_mem = _mem_heads(p[:, 2 * mixer_width:], gqm_ref[...], km_ref, vm_ref)
    ycat_ref[:, mixer_width:] = y_mem.astype(BF16)
    o_ref[0] = x + _dot(ycat_ref[...], wout_ref[...])


def _mixer_a(h, g_mix, w_in, g_sgu, w_s, b_s, km, vm, g_qm, w_out, *, tm):
    B, S, D = h.shape
    mixer_width = g_sgu.shape[0]
    n_in = w_in.shape[1]
    M = km.shape[1]
    bias = jnp.repeat(b_s.T, GROUP_DIM, axis=1)
    gqm = jnp.tile(g_qm, N_MEM_HEADS)[None]
    full = lambda shape: pl.BlockSpec(shape, lambda b, i: (0,) * len(shape))
    return pl.pallas_call(
        functools.partial(_mixer_a_kernel, mixer_width=mixer_width),
        grid=(B, S // tm),
        in_specs=[pl.BlockSpec((1, tm, D), lambda b, i: (b, i, 0)),
                  full((1, D)), full((D, n_in)), full((1, mixer_width)),
                  full(w_s.shape), full((CHUNK, mixer_width)),
                  pl.BlockSpec((1, M, MEM_WIDTH), lambda b, i: (b, 0, 0)),
                  pl.BlockSpec((1, M, MEM_WIDTH), lambda b, i: (b, 0, 0)),
                  full((1, MEM_WIDTH)), full((D, D))],
        out_specs=pl.BlockSpec((1, tm, D), lambda b, i: (b, i, 0)),
        out_shape=jax.ShapeDtypeStruct((B, S, D), F32),
        scratch_shapes=[pltpu.VMEM((tm, D), BF16)],
        compiler_params=pltpu.CompilerParams(
            dimension_semantics=("parallel", "parallel"), vmem_limit_bytes=VMEM_LIMIT),
        name="mixer_a",
    )(h, g_mix[None], w_in.astype(BF16), g_sgu[None], w_s, bias, km, vm, gqm, w_out.astype(BF16))


def _ffn_kernel(*refs, d_ff, ch, n_pre):
    h_ref, halo_ref = refs[:2]
    pre = refs[2:2 + 2 * n_pre]
    wout_ref = refs[2 + 2 * n_pre] if n_pre else None
    g_ref, wup_ref, cw_ref, cb_ref, wdn_ref, o_ref, a_ref = refs[len(refs) - 7:]
    i = pl.program_id(1)
    x = h_ref[0]
    xh = halo_ref[0]
    tm = x.shape[0]
    row0 = 0
    for j in range(n_pre):
        y_ref, yh_ref = pre[2 * j], pre[2 * j + 1]
        w = wout_ref[row0:row0 + y_ref.shape[2], :]
        row0 += y_ref.shape[2]
        x = x + _dot(y_ref[0], w)
        xh = xh + _dot(yh_ref[0], w)[BF16_ROWS - HALO:]
    g = g_ref[...]
    xn = _rms(x, g).astype(BF16)
    xh = _rms(xh, g)
    xh = jnp.where(i > 0, xh, 0.0).astype(BF16)
    xe = jnp.concatenate([xh, xn], axis=0)

    def conv_part(cols):
        hh = _dot(xe, wup_ref[:, cols])
        w = cw_ref[:, cols]
        return (w[2:3] * hh[HALO:] + w[1:2] * hh[HALO - 1:HALO - 1 + tm]
                + w[0:1] * hh[HALO - 2:HALO - 2 + tm] + cb_ref[:, cols])

    for c in range(d_ff // ch):
        gate = conv_part(slice(c * ch, (c + 1) * ch))
        val = conv_part(slice(d_ff + c * ch, d_ff + (c + 1) * ch))
        a_ref[:, c * ch:(c + 1) * ch] = (_gelu(gate) * val).astype(BF16)
    o_ref[0] = x + _dot(a_ref[...], wdn_ref[...])


def _ffn(h, g_ffn, w_up, conv_w, conv_b, w_down, *, tm, ch, pending=(), w_out=None):
    B, S, D = h.shape
    d_ff = w_down.shape[0]
    full = lambda shape: pl.BlockSpec(shape, lambda b, i: (0,) * len(shape), pipeline_mode=pl.Buffered(1))

    def tile_and_halo(width, halo_rows):
        per_tile = tm // halo_rows
        return [pl.BlockSpec((1, tm, width), lambda b, i: (b, i, 0)),
                pl.BlockSpec((1, halo_rows, width), lambda b, i: (b, jnp.maximum(i * per_tile - 1, 0), 0))]

    in_specs = tile_and_halo(D, HALO)
    args = [h, h]
    for y in pending:
        in_specs += tile_and_halo(y.shape[2], BF16_ROWS)
        args += [y, y]
    if pending:
        in_specs.append(full(w_out.shape))
        args.append(w_out.astype(BF16))
    in_specs += [full((1, D)), full((D, 2 * d_ff)), full((CONV_WIDTH, 2 * d_ff)),
                 full((1, 2 * d_ff)), full((d_ff, D))]
    args += [g_ffn[None], w_up.astype(BF16), conv_w, conv_b[None], w_down.astype(BF16)]
    return pl.pallas_call(
        functools.partial(_ffn_kernel, d_ff=d_ff, ch=ch, n_pre=len(pending)),
        grid=(B, S // tm),
        in_specs=in_specs,
        out_specs=pl.BlockSpec((1, tm, D), lambda b, i: (b, i, 0)),
        out_shape=jax.ShapeDtypeStruct((B, S, D), F32),
        scratch_shapes=[pltpu.VMEM((tm, d_ff), BF16)],
        compiler_params=pltpu.CompilerParams(
            dimension_semantics=("parallel", "parallel"), vmem_limit_bytes=VMEM_LIMIT),
        name="ffn",
    )(*args)


def _inproj_b_kernel(h_ref, g_ref, wqk_ref, wvt_ref, wmem_ref, gq_ref, gk_ref, km_ref, vm_ref,
                     gqm_ref, qhi_ref, qlo_ref, k_ref, vt_ref, kmean_ref, ymem_ref):
    x = h_ref[0]
    tm = x.shape[0]
    xn = _rms(x, g_ref[...]).astype(BF16)
    width = gq_ref.shape[1]
    n_heads = width // HEAD_DIM
    qk = _dot(xn, wqk_ref[...])
    q = _pair_rms(qk[:, :width]) * (gq_ref[...] * HEAD_DIM ** -0.5)
    k = _pair_rms(qk[:, width:]) * gk_ref[...]
    q_hi = q.astype(BF16)
    qhi_ref[0] = q_hi
    qlo_ref[0] = (q - q_hi.astype(F32)).astype(BF16)
    k_ref[0] = k.astype(BF16)
    nb = tm // MOBA_BLOCK
    kmean_ref[0] = jnp.concatenate(
        [jnp.mean(k[r * MOBA_BLOCK:(r + 1) * MOBA_BLOCK], axis=0, keepdims=True) for r in range(nb)],
        axis=0)
    vt = _nt(wvt_ref[...], xn)
    tail = jnp.where(lax.broadcasted_iota(jnp.int32, (BF16_ROWS, tm), 0) == 0, 1.0, 0.0)
    pieces = []
    for h in range(n_heads):
        pieces += [vt[h * HEAD_DIM:(h + 1) * HEAD_DIM], tail]
    vt = jnp.concatenate(pieces, axis=0).astype(BF16)
    for r in range(nb):
        vt_ref[0, r] = vt[:, r * MOBA_BLOCK:(r + 1) * MOBA_BLOCK]
    ymem_ref[0] = _mem_heads(_dot(xn, wmem_ref[...]), gqm_ref[...], km_ref, vm_ref).astype(BF16)


def _inproj_b(h, g_mix, w_in, g_q, g_k, km, vm, g_qm, *, tm):
    B, S, D = h.shape
    width = (w_in.shape[1] - MEM_WIDTH) // 3
    n_heads = width // HEAD_DIM
    M = km.shape[1]
    nt = S // tm
    nb = tm // MOBA_BLOCK
    w_qk = w_in[:, :2 * width].astype(BF16)
    w_vt = w_in[:, 2 * width:3 * width].T.astype(BF16)
    w_mem = w_in[:, 3 * width:].astype(BF16)
    gq = jnp.tile(g_q, n_heads)[None]
    gk = jnp.tile(g_k, n_heads)[None]
    gqm = jnp.tile(g_qm, N_MEM_HEADS)[None]
    full = lambda shape: pl.BlockSpec(shape, lambda b, i: (0,) * len(shape))
    tile = lambda w: pl.BlockSpec((1, tm, w), lambda b, i: (b, i, 0))
    outs = pl.pallas_call(
        _inproj_b_kernel,
        grid=(B, nt),
        in_specs=[tile(D), full((1, D)), full(w_qk.shape), full(w_vt.shape), full(w_mem.shape),
                  full((1, width)), full((1, width)),
                  pl.BlockSpec((1, M, MEM_WIDTH), lambda b, i: (b, 0, 0)),
                  pl.BlockSpec((1, M, MEM_WIDTH), lambda b, i: (b, 0, 0)),
                  full((1, MEM_WIDTH))],
        out_specs=[tile(width), tile(width), tile(width),
                   pl.BlockSpec((1, nb, n_heads * V_ROWS, MOBA_BLOCK), lambda b, i: (b, i, 0, 0)),
                   pl.BlockSpec((1, nb, width), lambda b, i: (b * nt + i, 0, 0)),
                   tile(MEM_WIDTH)],
        out_shape=[jax.ShapeDtypeStruct((B, S, width), BF16),
                   jax.ShapeDtypeStruct((B, S, width), BF16),
                   jax.ShapeDtypeStruct((B, S, width), BF16),
                   jax.ShapeDtypeStruct((B, S // MOBA_BLOCK, n_heads * V_ROWS, MOBA_BLOCK), BF16),
                   jax.ShapeDtypeStruct((B * nt, nb, width), F32),
                   jax.ShapeDtypeStruct((B, S, MEM_WIDTH), BF16)],
        compiler_params=pltpu.CompilerParams(
            dimension_semantics=("parallel", "parallel"), vmem_limit_bytes=VMEM_LIMIT),
        name="inproj_b",
    )(h, g_mix[None], w_qk, w_vt, w_mem, gq, gk, km, vm, gqm)
    q_hi, q_lo, k, vt, kmean, y_mem = outs
    return q_hi, q_lo, k, vt, kmean.reshape(B, S // MOBA_BLOCK, width), y_mem


def _moba_kernel(qhi_ref, qlo_ref, k_ref, vt_ref, kmean_ref, o_ref, bias_ref, s_ref, mx_ref, *,
                 group):
    own = pl.program_id(2)
    tq = qhi_ref.shape[1]
    nb = kmean_ref.shape[1]
    span = group * MOBA_BLOCK
    last_group = nb // group - 1
    rows = [slice(i * MOBA_BLOCK, (i + 1) * MOBA_BLOCK) for i in range(group)]
    cols = [slice(hd * tq, (hd + 1) * tq) for hd in range(2)]
    lane = lax.broadcasted_iota(jnp.int32, (1, LANES), 1)
    zero = jnp.zeros((), BF16)

    def both(x):
        return jnp.concatenate([jnp.where(lane < HEAD_DIM, x, zero), jnp.where(lane >= HEAD_DIM, x, zero)],
                               axis=0)

    q_both, q_lo = both(qhi_ref[0]), both(qlo_ref[0])
    km = kmean_ref[0]
    km_hi = km.astype(BF16)
    km_lo = (km - km_hi.astype(F32)).astype(BF16)

    def scores_into(g, i):
        start = pl.multiple_of((jnp.minimum(g, last_group) * group + i) * MOBA_BLOCK, MOBA_BLOCK)
        s = _nt(k_ref[0, pl.ds(start, MOBA_BLOCK), :], q_both)
        s_ref[rows[i], :] = s
        mx_ref[i:i + 1, :] = jnp.max(s, axis=0, keepdims=True)

    gate = _nt(km_hi, q_both) + (_nt(km_hi, q_lo) + _nt(km_lo, q_both))
    k_own = k_ref[0, pl.ds(pl.multiple_of(own * MOBA_BLOCK, MOBA_BLOCK), MOBA_BLOCK), :]
    s_own = _nt(k_own, q_both)
    for i in range(group - 1):
        scores_into(0, i)

    blk = lax.broadcasted_iota(jnp.int32, (nb, 2 * tq), 0)
    gate = jnp.where(blk < own, gate, NEG)
    sel = jnp.zeros(gate.shape, jnp.bool_)
    for j in range(MOBA_TOPK):
        mx = jnp.max(gate, axis=0, keepdims=True)
        first = jnp.min(jnp.where(gate == mx, blk, nb), axis=0, keepdims=True)
        pick = blk == first
        sel = jnp.logical_or(sel, jnp.logical_and(pick, j < own))
        gate = jnp.where(pick, -jnp.inf, gate)
    bias_ref[...] = jnp.where(sel, 0.0, NEG)

    key = lax.broadcasted_iota(jnp.int32, (MOBA_BLOCK, 2 * tq), 0)
    qry = lax.broadcasted_iota(jnp.int32, (MOBA_BLOCK, 2 * tq), 1)
    s_own = jnp.where(key <= jnp.where(qry < tq, qry, qry - tq), s_own, NEG)
    m0 = jnp.max(s_own, axis=0, keepdims=True)
    p_own = jnp.exp(s_own - m0).astype(BF16)
    init = tuple((m0[:, cols[hd]], _dot(vt_ref[0, own, hd * V_ROWS:(hd + 1) * V_ROWS, :], p_own[:, cols[hd]]))
                 for hd in range(2))

    def trip(g, state, refill, width=group):
        last = group - 1
        peeled = width == group
        if peeled:
            scores_into(g, last)
        main = range(last) if peeled else range(width)

        def probs_times_values(i, hd, shift):
            p = jnp.exp((s_ref[rows[i], cols[hd]] + shift).astype(BF16))
            return _dot(vt_ref[0, g * group + i, hd * V_ROWS:(hd + 1) * V_ROWS, :], p)

        bias, m_new, acc = [], [], []
        for hd in range(2):
            m, a = state[hd]
            b = [bias_ref[pl.ds(g * group + i, 1), cols[hd]] for i in range(width)]
            mx = m
            for i in main:
                mx = jnp.maximum(mx, mx_ref[i:i + 1, cols[hd]] + b[i])
            bias.append(b)
            m_new.append(mx)
            acc.append(jnp.exp(m - mx) * a)
        for i in main:
            for hd in range(2):
                acc[hd] = acc[hd] + probs_times_values(i, hd, bias[hd][i] - m_new[hd])
            if i < refill:
                scores_into(g + 1, i)
        if not peeled:
            return tuple(zip(m_new, acc))
        out = []
        for hd in range(2):
            mx = jnp.maximum(m_new[hd], mx_ref[last:group, cols[hd]] + bias[hd][last])
            a = jnp.exp(m_new[hd] - mx) * acc[hd]
            out.append((mx, a + probs_times_values(last, hd, bias[hd][last] - mx)))
        return tuple(out)

    n_trips = (own + group - 1) // group
    g_last = jnp.maximum(n_trips - 1, 0)
    short = own - g_last * group <= group // 2
    half = group // 2

    def once(cond, fn, st):
        return lax.fori_loop(0, jnp.where(cond, 1, 0), lambda _, s: fn(s), st)

    state = lax.fori_loop(0, n_trips - 2, functools.partial(trip, refill=group - 1), init)
    two = n_trips > 1
    state = once(jnp.logical_and(two, jnp.logical_not(short)), lambda s: trip(g_last - 1, s, group - 1), state)
    state = once(jnp.logical_and(two, short), lambda s: trip(g_last - 1, s, half), state)
    state = once(jnp.logical_and(n_trips > 0, jnp.logical_not(short)), lambda s: trip(g_last, s, 0), state)
    final = once(jnp.logical_and(n_trips > 0, short), lambda s: trip(g_last, s, 0, half), state)
    out_t = jnp.concatenate(
        [acc[:HEAD_DIM] / acc[HEAD_DIM:HEAD_DIM + 1] for _, acc in final], axis=0)
    o_ref[0] = out_t.T.astype(o_ref.dtype)


def _moba(q_hi, q_lo, k, vt, kmean, *, tq, group):
    B, S, width = q_hi.shape
    nb = S // MOBA_BLOCK
    n_pairs = width // LANES
    assert nb % group == 0 and tq == MOBA_BLOCK
    pair_tile = pl.BlockSpec((1, tq, LANES), lambda b, hp, t: (b, t, hp))
    return pl.pallas_call(
        functools.partial(_moba_kernel, group=group),
        grid=(B, n_pairs, S // tq),
        in_specs=[pair_tile, pair_tile,
                  pl.BlockSpec((1, S, LANES), lambda b, hp, t: (b, 0, hp)),
                  pl.BlockSpec((1, nb, 2 * V_ROWS, MOBA_BLOCK), lambda b, hp, t: (b, 0, hp, 0)),
                  pl.BlockSpec((1, nb, LANES), lambda b, hp, t: (b, 0, hp))],
        out_specs=pair_tile,
        out_shape=jax.ShapeDtypeStruct((B, S, width), BF16),
        scratch_shapes=[pltpu.VMEM((nb, 2 * tq), F32),
                        pltpu.VMEM((group * MOBA_BLOCK, 2 * tq), F32),
                        pltpu.VMEM((group, 2 * tq), F32)],
        compiler_params=pltpu.CompilerParams(
            dimension_semantics=("parallel", "parallel", "arbitrary"),
            vmem_limit_bytes=VMEM_LIMIT),
        name="moba",
    )(q_hi, q_lo, k, vt, kmean)


def kernel(x, mem, g_mix, g_ffn, w_in_a, w_out_a, g_sgu, w_s, b_s, w_in_b, w_out_b, g_q_b,
           g_k_b, g_mem, w_mem_kv, g_km, g_qm, w_up, conv_w, conv_b, w_down):
    B, S, D = x.shape
    depth = g_mix.shape[0]
    assert S % MOBA_BLOCK == 0
    km, vm = _mem_kv(mem, g_mem, w_mem_kv, g_km)
    h = x
    for i in range(depth):
        j = i // 2
        ffn = functools.partial(_ffn, g_ffn=g_ffn[i], w_up=w_up[i], conv_w=conv_w[i], conv_b=conv_b[i],
                                w_down=w_down[i], tm=1024, ch=256)
        if i % 2 == 0:
            h = _mixer_a(h, g_mix[i], w_in_a[j], g_sgu[j], w_s[j], b_s[j], km, vm, g_qm[i],
                         w_out_a[j], tm=512)
            h = ffn(h)
        else:
            q_hi, q_lo, k, vt, kmean, y_mem = _inproj_b(
                h, g_mix[i], w_in_b[j], g_q_b[j], g_k_b[j], km, vm, g_qm[i], tm=1024)
            y_attn = _moba(q_hi, q_lo, k, vt, kmean, tq=MOBA_BLOCK, group=8)
            h = ffn(h, pending=(y_attn, y_mem), w_out=w_out_b[j])
    return h
```

```python
import functools
import math

import jax
import jax.numpy as jnp
from jax import lax
from jax.experimental import pallas as pl
from jax.experimental.pallas import tpu as pltpu

F32 = jnp.float32
BF16 = jnp.bfloat16

HEAD_DIM = 64
LANES = 128
BF16_ROWS = 16
N_MEM_HEADS = 4
MEM_WIDTH = N_MEM_HEADS * HEAD_DIM
CHUNK = 128
GROUP_DIM = 128
MOBA_BLOCK = 256
MOBA_TOPK = 3
V_ROWS = HEAD_DIM + BF16_ROWS
CONV_WIDTH = 3
HALO = 8
EPS = 1e-6
NEG = -1e30
VMEM_LIMIT = 56 * 1024 * 1024


def _nt(a, b):
    return lax.dot_general(a, b, (((1,), (1,)), ((), ())), preferred_element_type=F32)


def _dot(a, b):
    return jnp.dot(a, b, preferred_element_type=F32)


def _gelu(x):
    c = math.sqrt(2.0 / math.pi)
    return 0.5 * x * (1.0 + jnp.tanh(c * (x + 0.044715 * (x * x * x))))


def _rms(x, g):
    return x * lax.rsqrt(jnp.mean(x * x, axis=-1, keepdims=True) + EPS) * g


def _pair_rms(x):
    lane = lax.broadcasted_iota(jnp.int32, (1, LANES), 1)
    lo = lane < HEAD_DIM
    outs = []
    for c in range(x.shape[1] // LANES):
        xc = x[:, c * LANES:(c + 1) * LANES]
        x2 = xc * xc
        s_lo = jnp.sum(jnp.where(lo, x2, 0.0), axis=-1, keepdims=True)
        s_hi = jnp.sum(jnp.where(lo, 0.0, x2), axis=-1, keepdims=True)
        ms = jnp.where(lo, s_lo, s_hi) * (1.0 / HEAD_DIM)
        outs.append(xc * lax.rsqrt(ms + EPS))
    return jnp.concatenate(outs, axis=-1)


def _mem_heads(qm, gq, km_ref, vm_ref):
    q = (_pair_rms(qm) * (gq * HEAD_DIM ** -0.5)).astype(BF16)
    outs = []
    for hh in range(N_MEM_HEADS):
        ls = slice(hh * HEAD_DIM, (hh + 1) * HEAD_DIM)
        s = _nt(q[:, ls], km_ref[0, :, ls])
        e = jnp.exp(s - jnp.max(s, axis=-1, keepdims=True))
        l = jnp.sum(e, axis=-1, keepdims=True)
        outs.append(_dot(e.astype(BF16), vm_ref[0, :, ls]) / l)
    return jnp.concatenate(outs, axis=-1)


def _mem_kv_kernel(mem_ref, g_ref, w_ref, gk_ref, km_ref, vm_ref):
    xn = _rms(mem_ref[0], g_ref[...]).astype(BF16)
    kv = _dot(xn, w_ref[...])
    km_ref[0] = (_pair_rms(kv[:, :MEM_WIDTH]) * gk_ref[...]).astype(BF16)
    vm_ref[0] = kv[:, MEM_WIDTH:].astype(BF16)


def _mem_kv(mem, g_mem, w_mem_kv, g_km):
    B, M, D = mem.shape
    gk = jnp.tile(g_km, N_MEM_HEADS)[None]
    full = lambda shape: pl.BlockSpec(shape, lambda b: (0,) * len(shape))
    return pl.pallas_call(
        _mem_kv_kernel,
        grid=(B,),
        in_specs=[pl.BlockSpec((1, M, D), lambda b: (b, 0, 0)),
                  full((1, D)), full((D, 2 * MEM_WIDTH)), full((1, MEM_WIDTH))],
        out_specs=[pl.BlockSpec((1, M, MEM_WIDTH), lambda b: (b, 0, 0))] * 2,
        out_shape=[jax.ShapeDtypeStruct((B, M, MEM_WIDTH), BF16)] * 2,
        name="mem_kv",
    )(mem, g_mem[None], w_mem_kv.astype(BF16), gk)


def _mixer_a_kernel(h_ref, g_ref, win_ref, gsgu_ref, ws_ref, bias_ref, km_ref, vm_ref,
                    gqm_ref, wout_ref, o_ref, ycat_ref, *, mixer_width):
    x = h_ref[0]
    tm = x.shape[0]
    n_groups = mixer_width // GROUP_DIM
    n_chunks = tm // CHUNK
    xn = _rms(x, g_ref[...]).astype(BF16)
    p = _dot(xn, win_ref[...])
    z = _gelu(p[:, :2 * mixer_width])
    u = z[:, :mixer_width]
    vn = _rms(z[:, mixer_width:], gsgu_ref[...]).astype(BF16)
    row = lax.broadcasted_iota(jnp.int32, (CHUNK, CHUNK), 0)
    col = lax.broadcasted_iota(jnp.int32, (CHUNK, CHUNK), 1)
    causal = row >= col
    for g in range(n_groups):
        gs = slice(g * GROUP_DIM, (g + 1) * GROUP_DIM)
        wc = jnp.where(causal, ws_ref[g], 0.0).astype(BF16)
        vg = jnp.concatenate([vn[c * CHUNK:(c + 1) * CHUNK, gs] for c in range(n_chunks)], axis=1)
        mixed = _dot(wc, vg)
        for c in range(n_chunks):
            rs = slice(c * CHUNK, (c + 1) * CHUNK)
            m_c = mixed[:, c * GROUP_DIM:(c + 1) * GROUP_DIM] + bias_ref[:, gs]
            ycat_ref[rs, gs] = (u[rs, gs] * m_c).astype(BF16)
    y_mem = _mem_heads(p[:, 2 * mixer_width:], gqm_ref[...], km_ref, vm_ref)
    ycat_ref[:, mixer_width:] = y_mem.astype(BF16)
    o_ref[0] = x + _dot(ycat_ref[...], wout_ref[...])


def _mixer_a(h, g_mix, w_in, g_sgu, w_s, b_s, km, vm, g_qm, w_out, *, tm):
    B, S, D = h.shape
    mixer_width = g_sgu.shape[0]
    n_in = w_in.shape[1]
    M = km.shape[1]
    bias = jnp.repeat(b_s.T, GROUP_DIM, axis=1)
    gqm = jnp.tile(g_qm, N_MEM_HEADS)[None]
    full = lambda shape: pl.BlockSpec(shape, lambda b, i: (0,) * len(shape))
    return pl.pallas_call(
        functools.partial(_mixer_a_kernel, mixer_width=mixer_width),
        grid=(B, S // tm),
        in_specs=[pl.BlockSpec((1, tm, D), lambda b, i: (b, i, 0)),
                  full((1, D)), full((D, n_in)), full((1, mixer_width)),
                  full(w_s.shape), full((CHUNK, mixer_width)),
                  pl.BlockSpec((1, M, MEM_WIDTH), lambda b, i: (b, 0, 0)),
                  pl.BlockSpec((1, M, MEM_WIDTH), lambda b, i: (b, 0, 0)),
                  full((1, MEM_WIDTH)), full((D, D))],
        out_specs=pl.BlockSpec((1, tm, D), lambda b, i: (b, i, 0)),
        out_shape=jax.ShapeDtypeStruct((B, S, D), F32),
        scratch_shapes=[pltpu.VMEM((tm, D), BF16)],
        compiler_params=pltpu.CompilerParams(
            dimension_semantics=("parallel", "parallel"), vmem_limit_bytes=VMEM_LIMIT),
        name="mixer_a",
    )(h, g_mix[None], w_in.astype(BF16), g_sgu[None], w_s, bias, km, vm, gqm, w_out.astype(BF16))


def _ffn_kernel(*refs, d_ff, ch, n_pre):
    h_ref, halo_ref = refs[:2]
    pre = refs[2:2 + 2 * n_pre]
    wout_ref = refs[2 + 2 * n_pre] if n_pre else None
    g_ref, wup_ref, cw_ref, cb_ref, wdn_ref, o_ref, a_ref = refs[len(refs) - 7:]
    i = pl.program_id(1)
    x = h_ref[0]
    xh = halo_ref[0]
    tm = x.shape[0]
    row0 = 0
    for j in range(n_pre):
        y_ref, yh_ref = pre[2 * j], pre[2 * j + 1]
        w = wout_ref[row0:row0 + y_ref.shape[2], :]
        row0 += y_ref.shape[2]
        x = x + _dot(y_ref[0], w)
        xh = xh + _dot(yh_ref[0], w)[BF16_ROWS - HALO:]
    g = g_ref[...]
    xn = _rms(x, g).astype(BF16)
    xh = _rms(xh, g)
    xh = jnp.where(i > 0, xh, 0.0).astype(BF16)
    xe = jnp.concatenate([xh, xn], axis=0)

    def conv_part(cols):
        hh = _dot(xe, wup_ref[:, cols])
        w = cw_ref[:, cols]
        return (w[2:3] * hh[HALO:] + w[1:2] * hh[HALO - 1:HALO - 1 + tm]
                + w[0:1] * hh[HALO - 2:HALO - 2 + tm] + cb_ref[:, cols])

    for c in range(d_ff // ch):
        gate = conv_part(slice(c * ch, (c + 1) * ch))
        val = conv_part(slice(d_ff + c * ch, d_ff + (c + 1) * ch))
        a_ref[:, c * ch:(c + 1) * ch] = (_gelu(gate) * val).astype(BF16)
    o_ref[0] = x + _dot(a_ref[...], wdn_ref[...])


def _ffn(h, g_ffn, w_up, conv_w, conv_b, w_down, *, tm, ch, pending=(), w_out=None):
    B, S, D = h.shape
    d_ff = w_down.shape[0]
    full = lambda shape: pl.BlockSpec(shape, lambda b, i: (0,) * len(shape), pipeline_mode=pl.Buffered(1))

    def tile_and_halo(width, halo_rows):
        per_tile = tm // halo_rows
        return [pl.BlockSpec((1, tm, width), lambda b, i: (b, i, 0)),
                pl.BlockSpec((1, halo_rows, width), lambda b, i: (b, jnp.maximum(i * per_tile - 1, 0), 0))]

    in_specs = tile_and_halo(D, HALO)
    args = [h, h]
    for y in pending:
        in_specs += tile_and_halo(y.shape[2], BF16_ROWS)
        args += [y, y]
    if pending:
        in_specs.append(full(w_out.shape))
        args.append(w_out.astype(BF16))
    in_specs += [full((1, D)), full((D, 2 * d_ff)), full((CONV_WIDTH, 2 * d_ff)),
                 full((1, 2 * d_ff)), full((d_ff, D))]
    args += [g_ffn[None], w_up.astype(BF16), conv_w, conv_b[None], w_down.astype(BF16)]
    return pl.pallas_call(
        functools.partial(_ffn_kernel, d_ff=d_ff, ch=ch, n_pre=len(pending)),
        grid=(B, S // tm),
        in_specs=in_specs,
        out_specs=pl.BlockSpec((1, tm, D), lambda b, i: (b, i, 0)),
        out_shape=jax.ShapeDtypeStruct((B, S, D), F32),
        scratch_shapes=[pltpu.VMEM((tm, d_ff), BF16)],
        compiler_params=pltpu.CompilerParams(
            dimension_semantics=("parallel", "parallel"), vmem_limit_bytes=VMEM_LIMIT),
        name="ffn",
    )(*args)


def _inproj_b_kernel(h_ref, g_ref, wqk_ref, wvt_ref, wmem_ref, gq_ref, gk_ref, km_ref, vm_ref,
                     gqm_ref, qhi_ref, qlo_ref, k_ref, vt_ref, kmean_ref, ymem_ref):
    x = h_ref[0]
    tm = x.shape[0]
    xn = _rms(x, g_ref[...]).astype(BF16)
    width = gq_ref.shape[1]
    n_heads = width // HEAD_DIM
    qk = _dot(xn, wqk_ref[...])
    q = _pair_rms(qk[:, :width]) * (gq_ref[...] * HEAD_DIM ** -0.5)
    k = _pair_rms(qk[:, width:]) * gk_ref[...]
    q_hi = q.astype(BF16)
    qhi_ref[0] = q_hi
    qlo_ref[0] = (q - q_hi.astype(F32)).astype(BF16)
    k_ref[0] = k.astype(BF16)
    nb = tm // MOBA_BLOCK
    kmean_ref[0] = jnp.concatenate(
        [jnp.mean(k[r * MOBA_BLOCK:(r + 1) * MOBA_BLOCK], axis=0, keepdims=True) for r in range(nb)],
        axis=0)
    vt = _nt(wvt_ref[...], xn)
    tail = jnp.where(lax.broadcasted_iota(jnp.int32, (BF16_ROWS, tm), 0) == 0, 1.0, 0.0)
    pieces = []
    for h in range(n_heads):
        pieces += [vt[h * HEAD_DIM:(h + 1) * HEAD_DIM], tail]
    vt = jnp.concatenate(pieces, axis=0).astype(BF16)
    for r in range(nb):
        vt_ref[0, r] = vt[:, r * MOBA_BLOCK:(r + 1) * MOBA_BLOCK]
    ymem_ref[0] = _mem_heads(_dot(xn, wmem_ref[...]), gqm_ref[...], km_ref, vm_ref).astype(BF16)


def _inproj_b(h, g_mix, w_in, g_q, g_k, km, vm, g_qm, *, tm):
    B, S, D = h.shape
    width = (w_in.shape[1] - MEM_WIDTH) // 3
    n_heads = width // HEAD_DIM
    M = km.shape[1]
    nt = S // tm
    nb = tm // MOBA_BLOCK
    w_qk = w_in[:, :2 * width].astype(BF16)
    w_vt = w_in[:, 2 * width:3 * width].T.astype(BF16)
    w_mem = w_in[:, 3 * width:].astype(BF16)
    gq = jnp.tile(g_q, n_heads)[None]
    gk = jnp.tile(g_k, n_heads)[None]
    gqm = jnp.tile(g_qm, N_MEM_HEADS)[None]
    full = lambda shape: pl.BlockSpec(shape, lambda b, i: (0,) * len(shape))
    tile = lambda w: pl.BlockSpec((1, tm, w), lambda b, i: (b, i, 0))
    outs = pl.pallas_call(
        _inproj_b_kernel,
        grid=(B, nt),
        in_specs=[tile(D), full((1, D)), full(w_qk.shape), full(w_vt.shape), full(w_mem.shape),
                  full((1, width)), full((1, width)),
                  pl.BlockSpec((1, M, MEM_WIDTH), lambda b, i: (b, 0, 0)),
                  pl.BlockSpec((1, M, MEM_WIDTH), lambda b, i: (b, 0, 0)),
                  full((1, MEM_WIDTH))],
        out_specs=[tile(width), tile(width), tile(width),
                   pl.BlockSpec((1, nb, n_heads * V_ROWS, MOBA_BLOCK), lambda b, i: (b, i, 0, 0)),
                   pl.BlockSpec((1, nb, width), lambda b, i: (b * nt + i, 0, 0)),
                   tile(MEM_WIDTH)],
        out_shape=[jax.ShapeDtypeStruct((B, S, width), BF16),
                   jax.ShapeDtypeStruct((B, S, width), BF16),
                   jax.ShapeDtypeStruct((B, S, width), BF16),
                   jax.ShapeDtypeStruct((B, S // MOBA_BLOCK, n_heads * V_ROWS, MOBA_BLOCK), BF16),
                   jax.ShapeDtypeStruct((B * nt, nb, width), F32),
                   jax.ShapeDtypeStruct((B, S, MEM_WIDTH), BF16)],
        compiler_params=pltpu.CompilerParams(
            dimension_semantics=("parallel", "parallel"), vmem_limit_bytes=VMEM_LIMIT),
        name="inproj_b",
    )(h, g_mix[None], w_qk, w_vt, w_mem, gq, gk, km, vm, gqm)
    q_hi, q_lo, k, vt, kmean, y_mem = outs
    return q_hi, q_lo, k, vt, kmean.reshape(B, S // MOBA_BLOCK, width), y_mem


def _moba_kernel(qhi_ref, qlo_ref, k_ref, vt_ref, kmean_ref, o_ref, bias_ref, s_ref, mx_ref, *,
                 group):
    own = pl.program_id(2)
    tq = qhi_ref.shape[1]
    nb = kmean_ref.shape[1]
    last_group = nb // group - 1
    rows = [slice(i * MOBA_BLOCK, (i + 1) * MOBA_BLOCK) for i in range(group)]
    cols = [slice(hd * tq, (hd + 1) * tq) for hd in range(2)]
    lane = lax.broadcasted_iota(jnp.int32, (1, LANES), 1)
    zero = jnp.zeros((), BF16)

    def both(x):
        return jnp.concatenate([jnp.where(lane < HEAD_DIM, x, zero), jnp.where(lane >= HEAD_DIM, x, zero)],
                               axis=0)

    q_both, q_lo = both(qhi_ref[0]), both(qlo_ref[0])
    km = kmean_ref[0]
    km_hi = km.astype(BF16)
    km_lo = (km - km_hi.astype(F32)).astype(BF16)

    def scores_into(g, i):
        start = pl.multiple_of((jnp.minimum(g, last_group) * group + i) * MOBA_BLOCK, MOBA_BLOCK)
        s = _nt(k_ref[0, pl.ds(start, MOBA_BLOCK), :], q_both)
        s_ref[rows[i], :] = s
        mx_ref[i:i + 1, :] = jnp.max(s, axis=0, keepdims=True)

    gate = _nt(jnp.concatenate([km_hi, km_hi, km_lo], axis=1),
               jnp.concatenate([q_both, q_lo, q_both], axis=1))
    k_own = k_ref[0, pl.ds(pl.multiple_of(own * MOBA_BLOCK, MOBA_BLOCK), MOBA_BLOCK), :]
    s_own = _nt(k_own, q_both)
    for i in range(group - 1):
        scores_into(0, i)

    blk = lax.broadcasted_iota(jnp.int32, (nb, 2 * tq), 0)
    gate = jnp.where(blk < own, gate, NEG)
    sel = jnp.zeros(gate.shape, jnp.bool_)
    for j in range(MOBA_TOPK):
        mx = jnp.max(gate, axis=0, keepdims=True)
        first = jnp.min(jnp.where(gate == mx, blk, nb), axis=0, keepdims=True)
        pick = blk == first
        sel = jnp.logical_or(sel, jnp.logical_and(pick, j < own))
        gate = jnp.where(pick, -jnp.inf, gate)
    bias_ref[...] = jnp.where(sel, 0.0, NEG)

    key = lax.broadcasted_iota(jnp.int32, (MOBA_BLOCK, 2 * tq), 0)
    qry = lax.broadcasted_iota(jnp.int32, (MOBA_BLOCK, 2 * tq), 1)
    s_own = jnp.where(key <= jnp.where(qry < tq, qry, qry - tq), s_own, NEG)
    m0 = jnp.max(s_own, axis=0, keepdims=True)
    p_own = jnp.exp(s_own - m0).astype(BF16)
    init = tuple((m0[:, cols[hd]], _dot(vt_ref[0, own, hd * V_ROWS:(hd + 1) * V_ROWS, :], p_own[:, cols[hd]]))
                 for hd in range(2))

    def trip(g, state, refill, width=group):
        last = group - 1
        peeled = width == group
        if peeled:
            scores_into(g, last)
        main = range(last) if peeled else range(width)

        def probs_times_values(i, hd, shift):
            p = jnp.exp((s_ref[rows[i], cols[hd]] + shift).astype(BF16))
            return _dot(vt_ref[0, g * group + i, hd * V_ROWS:(hd + 1) * V_ROWS, :], p)

        bias, m_new, acc = [], [], []
        for hd in range(2):
            m, a = state[hd]
            b = [bias_ref[pl.ds(g * group + i, 1), cols[hd]] for i in range(width)]
            mx = m
            for i in main:
                mx = jnp.maximum(mx, mx_ref[i:i + 1, cols[hd]] + b[i])
            bias.append(b)
            m_new.append(mx)
            acc.append(jnp.exp(m - mx) * a)
        for i in main:
            for hd in range(2):
                acc[hd] = acc[hd] + probs_times_values(i, hd, bias[hd][i] - m_new[hd])
            if i < refill:
                scores_into(g + 1, i)
        if not peeled:
            return tuple(zip(m_new, acc))
        out = []
        for hd in range(2):
            mx = jnp.maximum(m_new[hd], mx_ref[last:group, cols[hd]] + bias[hd][last])
            a = jnp.exp(m_new[hd] - mx) * acc[hd]
            out.append((mx, a + probs_times_values(last, hd, bias[hd][last] - mx)))
        return tuple(out)

    n_trips = (own + group - 1) // group
    g_last = jnp.maximum(n_trips - 1, 0)
    quarter = group // 4
    quarters = (own - g_last * group + quarter - 1) // quarter
    short = quarters <= 2

    def once(cond, fn, st):
        return lax.fori_loop(0, jnp.where(cond, 1, 0), lambda _, s: fn(s), st)

    state = lax.fori_loop(0, n_trips - 2, functools.partial(trip, refill=group - 1), init)
    two = n_trips > 1
    state = once(jnp.logical_and(two, jnp.logical_not(short)), lambda s: trip(g_last - 1, s, group - 1), state)
    state = once(jnp.logical_and(two, short), lambda s: trip(g_last - 1, s, 2 * quarter), state)
    for n in range(1, 5):
        state = once(jnp.logical_and(n_trips > 0, quarters == n),
                     lambda s, n=n: trip(g_last, s, 0, n * quarter), state)
    final = state
    out_t = jnp.concatenate(
        [acc[:HEAD_DIM] / acc[HEAD_DIM:HEAD_DIM + 1] for _, acc in final], axis=0)
    o_ref[0] = out_t.T.astype(o_ref.dtype)


def _moba(q_hi, q_lo, k, vt, kmean, *, tq, group):
    B, S, width = q_hi.shape
    nb = S // MOBA_BLOCK
    n_pairs = width // LANES
    assert nb % group == 0 and tq == MOBA_BLOCK
    pair_tile = pl.BlockSpec((1, tq, LANES), lambda b, hp, t: (b, t, hp))
    return pl.pallas_call(
        functools.partial(_moba_kernel, group=group),
        grid=(B, n_pairs, S // tq),
        in_specs=[pair_tile, pair_tile,
                  pl.BlockSpec((1, S, LANES), lambda b, hp, t: (b, 0, hp)),
                  pl.BlockSpec((1, nb, 2 * V_ROWS, MOBA_BLOCK), lambda b, hp, t: (b, 0, hp, 0)),
                  pl.BlockSpec((1, nb, LANES), lambda b, hp, t: (b, 0, hp))],
        out_specs=pair_tile,
        out_shape=jax.ShapeDtypeStruct((B, S, width), BF16),
        scratch_shapes=[pltpu.VMEM((nb, 2 * tq), F32),
                        pltpu.VMEM((group * MOBA_BLOCK, 2 * tq), F32),
                        pltpu.VMEM((group, 2 * tq), F32)],
        compiler_params=pltpu.CompilerParams(
            dimension_semantics=("parallel", "parallel", "arbitrary"),
            vmem_limit_bytes=VMEM_LIMIT),
        name="moba",
    )(q_hi, q_lo, k, vt, kmean)


def kernel(x, mem, g_mix, g_ffn, w_in_a, w_out_a, g_sgu, w_s, b_s, w_in_b, w_out_b, g_q_b,
           g_k_b, g_mem, w_mem_kv, g_km, g_qm, w_up, conv_w, conv_b, w_down):
    B, S, D = x.shape
    depth = g_mix.shape[0]
    assert S % MOBA_BLOCK == 0
    km, vm = _mem_kv(mem, g_mem, w_mem_kv, g_km)
    h = x
    for i in range(depth):
        j = i // 2
        ffn = functools.partial(_ffn, g_ffn=g_ffn[i], w_up=w_up[i], conv_w=conv_w[i], conv_b=conv_b[i],
                                w_down=w_down[i], tm=1024, ch=256)
        if i % 2 == 0:
            h = _mixer_a(h, g_mix[i], w_in_a[j], g_sgu[j], w_s[j], b_s[j], km, vm, g_qm[i],
                         w_out_a[j], tm=512)
            h = ffn(h)
        else:
            q_hi, q_lo, k, vt, kmean, y_mem = _inproj_b(
                h, g_mix[i], w_in_b[j], g_q_b[j], g_k_b[j], km, vm, g_qm[i], tm=1024)
            y_attn = _moba(q_hi, q_lo, k, vt, kmean, tq=MOBA_BLOCK, group=8)
            h = ffn(h, pending=(y_attn, y_mem), w_out=w_out_b[j])
    return h
```
